```python
import math
import jax, jax.numpy as jnp
from jax import lax
import numpy as np

D_MODEL = 2048
BATCH = 2
SEQ = 8192
DEPTH = 4

GRID_W = 64
CTX_LEN = 256
N_MIXERS = 2
N_SSD = (DEPTH + 1) // 2
N_MLA = DEPTH // 2

SSD_EXPAND = 2
D_INNER = SSD_EXPAND * D_MODEL
SSD_HEAD_DIM = 64
SSD_HEADS = D_INNER // SSD_HEAD_DIM
SSD_GROUPS = 8
SSD_STATE = 128
SSD_CONV = 5
SSD_CHUNK = 128
SSD_GN = SSD_GROUPS * SSD_STATE
CONV_CH = D_INNER + 2 * SSD_GN
SSD_IN = 2 * D_INNER + 2 * SSD_GN + 2 * SSD_HEADS
DT_MIN = 1e-3
DT_MAX = 1e-1

MLA_HEADS = 16
Q_RANK = 512
KV_RANK = 512
QK_NOPE = 128
QK_ROPE = 64
V_DIM = 128
MLA_GATE = MLA_HEADS * V_DIM
MLA_IN = Q_RANK + KV_RANK + QK_ROPE + MLA_GATE
ATTN_BLOCK = 128
ROPE_BASE = 10000.0

ALPHA = (2 * DEPTH) ** 0.25
BETA = (8 * DEPTH) ** -0.25
LN_EPS = 1e-5
RMS_EPS = 1e-6

kernel_name = 'hybrid_ssd_mla_deepnorm_prefix'


def layer_norm(x, g, b):
    xf = x.astype(jnp.float32)
    mu = jnp.mean(xf, -1, keepdims=True)
    var = jnp.mean(jnp.square(xf - mu), -1, keepdims=True)
    y = (xf - mu) * lax.rsqrt(var + LN_EPS) * g.astype(jnp.float32) + b.astype(jnp.float32)
    return y.astype(x.dtype)


def rms_norm(x, g):
    xf = x.astype(jnp.float32)
    y = xf * lax.rsqrt(jnp.mean(jnp.square(xf), -1, keepdims=True) + RMS_EPS)
    return (y * g.astype(jnp.float32)).astype(x.dtype)


def depthwise_conv(u, w, b):
    pad = SSD_CONV // 2
    y = lax.conv_general_dilated(u, w[:, None, :].astype(u.dtype), window_strides=(1,),
                                 padding=[(pad, pad)], dimension_numbers=('NWC', 'WIO', 'NWC'),
                                 feature_group_count=u.shape[-1])
    return y + b.astype(u.dtype)


def axial_rope_angles(n_tokens):
    rows = n_tokens // GRID_W
    rr, cc = jnp.meshgrid(jnp.arange(rows, dtype=jnp.float32), jnp.arange(GRID_W, dtype=jnp.float32), indexing='ij')
    n_freq = QK_ROPE // 4
    inv_freq = ROPE_BASE ** (-jnp.arange(n_freq, dtype=jnp.float32) / n_freq)
    ang = jnp.concatenate([rr.reshape(-1)[:, None] * inv_freq, cc.reshape(-1)[:, None] * inv_freq], -1)
    return jnp.cos(ang), jnp.sin(ang)


def apply_rope(t, cos, sin):
    tp = t.reshape(t.shape[:-1] + (QK_ROPE // 2, 2))
    t1, t2 = tp[..., 0], tp[..., 1]
    cos = cos.astype(t.dtype)
    sin = sin.astype(t.dtype)
    return jnp.stack([t1 * cos - t2 * sin, t1 * sin + t2 * cos], -1).reshape(t.shape)


def ssd_scan(xh, dt, A, Bm, Cm, h0):
    b, L, H, P = xh.shape
    G, N = Bm.shape[-2], Bm.shape[-1]
    E = H // G
    Q = SSD_CHUNK
    nc = L // Q
    x = (xh * dt[..., None]).reshape(b, nc, Q, G, E, P)
    a = jnp.moveaxis((dt * A).reshape(b, nc, Q, G, E), 2, -1)
    a_cs = jnp.cumsum(a, axis=-1)
    Bc = Bm.reshape(b, nc, Q, G, N)
    Cc = Cm.reshape(b, nc, Q, G, N)
    lower = jnp.tril(jnp.ones((Q, Q), dtype=bool))
    seg = a_cs[..., :, None] - a_cs[..., None, :]
    decay_in = jnp.exp(jnp.where(lower, seg, -jnp.inf))
    cb = jnp.einsum('bclgn,bcsgn->bcgls', Cc, Bc)
    y_diag = jnp.einsum('bcgels,bcsgep->bclgep', cb[:, :, :, None] * decay_in, x)
    to_end = jnp.moveaxis(jnp.exp(a_cs[..., -1:] - a_cs), -1, 2)
    chunk_states = jnp.einsum('bclgn,bclgep->bcgepn', Bc, x * to_end[..., None])
    chunk_decay = jnp.exp(a_cs[..., -1])

    def step(h, inp):
        s, d = inp
        return h * d[..., None, None] + s, h

    h_last, h_in = lax.scan(step, h0.reshape(b, G, E, P, N),
                            (jnp.moveaxis(chunk_states, 1, 0), jnp.moveaxis(chunk_decay, 1, 0)))
    h_in = jnp.moveaxis(h_in, 0, 1)
    from_start = jnp.moveaxis(jnp.exp(a_cs), -1, 2)
    y_off = jnp.einsum('bclgn,bcgepn->bclgep', Cc, h_in) * from_start[..., None]
    y = (y_diag + y_off).reshape(b, L, H, P)
    return y, h_last.reshape(b, H, P, N)


def ssd_mixer(u_x, u_c, w_in, conv_w, conv_b, dt_bias, a_log, d_skip, norm_w, w_out, need_ctx):
    A = (-jnp.exp(a_log.astype(jnp.float32))).astype(u_x.dtype)

    def project(u):
        b, L = u.shape[:2]
        h = u @ w_in
        z = h[..., :D_INNER]
        xbc = jax.nn.silu(depthwise_conv(h[..., D_INNER:D_INNER + CONV_CH], conv_w, conv_b))
        dt = jax.nn.softplus(h[..., D_INNER + CONV_CH:].reshape(b, L, 2, SSD_HEADS) + dt_bias)
        xh = xbc[..., :D_INNER].reshape(b, L, SSD_HEADS, SSD_HEAD_DIM)
        Bm = xbc[..., D_INNER:D_INNER + SSD_GN].reshape(b, L, SSD_GROUPS, SSD_STATE)
        Cm = xbc[..., D_INNER + SSD_GN:].reshape(b, L, SSD_GROUPS, SSD_STATE)
        return z, xh, dt, Bm, Cm

    def run(p, d, h0, reverse):
        _, xh, dt, Bm, Cm = p
        f = (lambda t: jnp.flip(t, 1)) if reverse else (lambda t: t)
        y, h = ssd_scan(f(xh), f(dt[:, :, d]), A[d], f(Bm), f(Cm), h0)
        return f(y), h

    def finish(p, y_f, y_b):
        z, xh, _, _, _ = p
        b, L = z.shape[:2]
        y = (y_f + y_b + d_skip[:, None] * xh).reshape(b, L, D_INNER) * jax.nn.silu(z)
        y = rms_norm(y.reshape(b, L, SSD_GROUPS, D_INNER // SSD_GROUPS),
                     norm_w.reshape(SSD_GROUPS, D_INNER // SSD_GROUPS)).reshape(b, L, D_INNER)
        return y @ w_out

    pc = project(u_c)
    px = project(u_x)
    h0 = jnp.zeros((u_x.shape[0], SSD_HEADS, SSD_HEAD_DIM, SSD_STATE), u_x.dtype)
    yc_f, hc_f = run(pc, 0, h0, False)
    yc_b, hc_b = run(pc, 1, h0, True)
    yx_f, _ = run(px, 0, hc_f, False)
    yx_b, _ = run(px, 1, hc_b, True)
    out_x = finish(px, yx_f, yx_b)
    out_c = finish(pc, yc_f, yc_b) if need_ctx else None
    return out_x, out_c


def mla_attend(qn, qr, kn, kr, v):
    scale = (QK_NOPE + QK_ROPE) ** -0.5
    s = (jnp.einsum('bhqd,bhkd->bhqk', qn, kn) + jnp.einsum('bhqr,bkr->bhqk', qr, kr)) * scale
    p = jax.nn.softmax(s.astype(jnp.float32), axis=-1).astype(v.dtype)
    return jnp.einsum('bhqk,bhkd->bhqd', p, v)


def mla_mixer(u_x, u_c, w_in, q_norm, kv_norm, w_uq, w_ukv, w_o, need_ctx):
    b, L = u_x.shape[:2]
    Lc = u_c.shape[1]
    cos, sin = axial_rope_angles(L)

    def project(u):
        bb, n = u.shape[:2]
        h = u @ w_in
        cq = rms_norm(h[..., :Q_RANK], q_norm)
        ckv = rms_norm(h[..., Q_RANK:Q_RANK + KV_RANK], kv_norm)
        kr = h[..., Q_RANK + KV_RANK:Q_RANK + KV_RANK + QK_ROPE]
        gate = h[..., Q_RANK + KV_RANK + QK_ROPE:]
        q = (cq @ w_uq).reshape(bb, n, MLA_HEADS, QK_NOPE + QK_ROPE)
        kv = (ckv @ w_ukv).reshape(bb, n, MLA_HEADS, QK_NOPE + V_DIM)
        return q[..., :QK_NOPE], q[..., QK_NOPE:], kv[..., :QK_NOPE], kr, kv[..., QK_NOPE:], gate

    qn, qr, kn, kr, v, gate = project(u_x)
    qr = apply_rope(qr, cos[:, None], sin[:, None])
    kr = apply_rope(kr, cos, sin)
    qn_c, qr_c, kn_c, kr_c, v_c, gate_c = project(u_c)

    kn_all = jnp.concatenate([kn_c, kn], 1).transpose(0, 2, 1, 3)
    kr_all = jnp.concatenate([kr_c, kr], 1)
    v_all = jnp.concatenate([v_c, v], 1).transpose(0, 2, 1, 3)
    nb = L // ATTN_BLOCK
    qn_b = qn.reshape(b, nb, ATTN_BLOCK, MLA_HEADS, QK_NOPE).transpose(1, 0, 3, 2, 4)
    qr_b = qr.reshape(b, nb, ATTN_BLOCK, MLA_HEADS, QK_ROPE).transpose(1, 0, 3, 2, 4)
    o = lax.map(lambda qs: mla_attend(qs[0], qs[1], kn_all, kr_all, v_all), (qn_b, qr_b))
    o = o.transpose(1, 0, 3, 2, 4).reshape(b, L, MLA_GATE)
    out_x = (jax.nn.silu(gate) * o) @ w_o

    out_c = None
    if need_ctx:
        o_c = mla_attend(qn_c.transpose(0, 2, 1, 3), qr_c.transpose(0, 2, 1, 3),
                         kn_c.transpose(0, 2, 1, 3), kr_c, v_c.transpose(0, 2, 1, 3))
        o_c = o_c.transpose(0, 2, 1, 3).reshape(b, Lc, MLA_GATE)
        out_c = (jax.nn.silu(gate_c) * o_c) @ w_o
    return out_x, out_c


def setup_inputs(seed: int = 0) -> dict:
    key = jax.random.key(seed)
    ks = iter(jax.random.split(key, 40))

    def nrm(shape, scale):
        return jax.random.normal(next(ks), shape, jnp.float32) * scale

    def unif(shape, lo, hi):
        return jax.random.uniform(next(ks), shape, jnp.float32, lo, hi)

    dt0 = jnp.exp(unif((N_SSD, 2, SSD_HEADS), math.log(DT_MIN), math.log(DT_MAX)))
    return {
        'x': nrm((BATCH, SEQ, D_MODEL), 1.0),
        'c': nrm((BATCH, D_MODEL), 1.0),
        'ctx': nrm((BATCH, CTX_LEN, D_MODEL), 1.0),
        'c_ctx': nrm((D_MODEL,), 1.0),
        'ada_w': nrm((DEPTH, D_MODEL, 3 * D_MODEL), D_MODEL ** -0.5),
        'ada_b': nrm((DEPTH, 3 * D_MODEL), 0.02),
        'ln_g': 1.0 + nrm((DEPTH, D_MODEL), 0.05),
        'ln_b': nrm((DEPTH, D_MODEL), 0.02),
        'ssd_w_in': nrm((N_SSD, D_MODEL, SSD_IN), D_MODEL ** -0.5),
        'ssd_conv_w': nrm((N_SSD, SSD_CONV, CONV_CH), SSD_CONV ** -0.5),
        'ssd_conv_b': nrm((N_SSD, CONV_CH), 0.02),
        'ssd_dt_bias': dt0 + jnp.log(-jnp.expm1(-dt0)),
        'ssd_a_log': jnp.log(unif((N_SSD, 2, SSD_HEADS), 1.0, 16.0)),
        'ssd_d': 1.0 + nrm((N_SSD, SSD_HEADS), 0.1),
        'ssd_norm_w': 1.0 + nrm((N_SSD, D_INNER), 0.05),
        'ssd_w_out': nrm((N_SSD, D_INNER, D_MODEL), BETA * D_INNER ** -0.5),
        'mla_w_in': nrm((N_MLA, D_MODEL, MLA_IN), D_MODEL ** -0.5),
        'mla_q_norm': 1.0 + nrm((N_MLA, Q_RANK), 0.05),
        'mla_kv_norm': 1.0 + nrm((N_MLA, KV_RANK), 0.05),
        'mla_w_uq': nrm((N_MLA, Q_RANK, MLA_HEADS * (QK_NOPE + QK_ROPE)), Q_RANK ** -0.5),
        'mla_w_ukv': nrm((N_MLA, KV_RANK, MLA_HEADS * (QK_NOPE + V_DIM)), KV_RANK ** -0.5),
        'mla_w_o': nrm((N_MLA, MLA_GATE, D_MODEL), BETA * MLA_GATE ** -0.5),
    }


def reference(x, c, ctx, c_ctx, ada_w, ada_b, ln_g, ln_b,
              ssd_w_in, ssd_conv_w, ssd_conv_b, ssd_dt_bias, ssd_a_log, ssd_d, ssd_norm_w, ssd_w_out,
              mla_w_in, mla_q_norm, mla_kv_norm, mla_w_uq, mla_w_ukv, mla_w_o):
    for i in range(DEPTH):
        need_ctx = i < DEPTH - 1
        mod_x = jax.nn.silu(c) @ ada_w[i] + ada_b[i]
        mod_c = jax.nn.silu(c_ctx) @ ada_w[i] + ada_b[i]
        sh_x, sc_x, g_x = jnp.split(mod_x[:, None, :], 3, axis=-1)
        sh_c, sc_c, g_c = jnp.split(mod_c, 3, axis=-1)
        u_x = x * (1.0 + sc_x) + sh_x
        u_c = ctx * (1.0 + sc_c) + sh_c
        j = i // N_MIXERS
        if i % N_MIXERS == 0:
            y_x, y_c = ssd_mixer(u_x, u_c, ssd_w_in[j], ssd_conv_w[j], ssd_conv_b[j], ssd_dt_bias[j],
                                 ssd_a_log[j], ssd_d[j], ssd_norm_w[j], ssd_w_out[j], need_ctx)
        else:
            y_x, y_c = mla_mixer(u_x, u_c, mla_w_in[j], mla_q_norm[j], mla_kv_norm[j],
                                 mla_w_uq[j], mla_w_ukv[j], mla_w_o[j], need_ctx)
        x = layer_norm(ALPHA * x + g_x * y_x, ln_g[i], ln_b[i])
        if need_ctx:
            ctx = layer_norm(ALPHA * ctx + g_c * y_c, ln_g[i], ln_b[i])
    return x
```

```python
import functools
import math

import jax
import jax.numpy as jnp
from jax import lax
from jax.experimental import pallas as pl
from jax.experimental.pallas import tpu as pltpu

F32 = jnp.float32
BF16 = jnp.bfloat16

GRID_W = 64
SSD_HEAD_DIM = 64
SSD_GROUPS = 8
SSD_STATE = 128
SSD_CHUNK = 128
MLA_HEADS = 16
QK_NOPE = 128
QK_ROPE = 64
V_DIM = 128
ROPE_BASE = 10000.0
LN_EPS = 1e-5
RMS_EPS = 1e-6

VMEM_LIMIT_BYTES = 56 * 1024 * 1024
LANES = 128

ROW_TILE = 768
LATENT_TILE = 512
CONV_TILE = 256
SCAN_TILE = 256
ATTN_Q_TILE = 512
ATTN_K_TILE = 768


def _params(n_axes):
    return pltpu.CompilerParams(dimension_semantics=("arbitrary",) * n_axes,
                                vmem_limit_bytes=VMEM_LIMIT_BYTES)


def _silu(v):
    return v * (1.0 / (1.0 + jnp.exp(-v)))


def _softplus(v):
    return jnp.maximum(v, 0.0) + jnp.log(1.0 + jnp.exp(-jnp.abs(v)))


def _dot(a, b):
    return jnp.dot(a, b, preferred_element_type=F32)


def _dot_nt(a, b):
    return lax.dot_general(a, b, (((1,), (1,)), ((), ())), preferred_element_type=F32)


def _mod_kernel(c_ref, w_ref, b_ref, o_ref):
    a = _silu(c_ref[...])
    o_ref[...] = jnp.dot(a, w_ref[...], preferred_element_type=F32,
                         precision=lax.Precision.HIGHEST) + b_ref[...]


def _modulation(c_rows, ada_w, ada_b):
    depth, d, n = ada_w.shape
    tn = 768 if n % 768 == 0 else n
    return pl.pallas_call(
        _mod_kernel,
        grid=(depth, n // tn),
        in_specs=[pl.BlockSpec((8, d), lambda i, j: (0, 0)),
                  pl.BlockSpec((None, d, tn), lambda i, j: (i, 0, j)),
                  pl.BlockSpec((None, 1, tn), lambda i, j: (i, 0, j))],
        out_specs=pl.BlockSpec((None, 8, tn), lambda i, j: (i, 0, j)),
        out_shape=jax.ShapeDtypeStruct((depth, 8, n), F32),
        compiler_params=_params(2),
        name="adaln_modulation",
    )(c_rows, ada_w, ada_b.reshape(depth, 1, n))


def _modulate(x_ref, mod_ref, u_ref, row0, n_latent):
    tm = x_ref.shape[0]
    rows = row0 + lax.broadcasted_iota(jnp.int32, (tm, 1), 0)
    is_ctx = rows >= n_latent
    shift = jnp.where(is_ctx, mod_ref[0:1, :], mod_ref[3:4, :])
    scale = jnp.where(is_ctx, mod_ref[1:2, :], mod_ref[4:5, :])
    u_ref[...] = (x_ref[...] * (1.0 + scale) + shift).astype(BF16)


def _inproj_kernel(x_ref, mod_ref, w_ref, *rest, n_latent, epilogue):
    o_ref, u_ref = rest[-2], rest[-1]
    extra = rest[:-2]
    i, j = pl.program_id(1), pl.program_id(2)
    tm = x_ref.shape[0]

    @pl.when(j == 0)
    def _():
        _modulate(x_ref, mod_ref, u_ref, i * tm, n_latent)

    acc = _dot(u_ref[...], w_ref[...])
    if epilogue == "silu":
        acc = _silu(acc)
    elif epilogue == "rmsnorm":
        g_ref, = extra
        ms = jnp.mean(acc * acc, axis=-1, keepdims=True)
        acc = acc * lax.rsqrt(ms + RMS_EPS) * g_ref[...]
    elif epilogue == "rope":
        cos_ref, sin_ref = extra
        acc = acc * cos_ref[...] + pltpu.roll(acc, 64, 1) * sin_ref[...]
    o_ref[...] = acc.astype(o_ref.dtype)


def _inproj(x, mod, w, n_latent, *, tn, epilogue="none", extra=(), out_dtype=BF16, name):
    b, lt, d = x.shape
    n = w.shape[1]
    tm = ROW_TILE
    tn = min(tn, n)
    extra_specs = []
    for e in extra:
        if e.shape[0] == 1:
            extra_specs.append(pl.BlockSpec((1, tn), lambda bi, i, j: (0, j)))
        else:
            extra_specs.append(pl.BlockSpec((tm, tn), lambda bi, i, j: (i, j)))
    return pl.pallas_call(
        functools.partial(_inproj_kernel, n_latent=n_latent, epilogue=epilogue),
        grid=(b, lt // tm, n // tn),
        in_specs=[pl.BlockSpec((None, tm, d), lambda bi, i, j: (bi, i, 0)),
                  pl.BlockSpec((None, 8, d), lambda bi, i, j: (bi, 0, 0)),
                  pl.BlockSpec((d, tn), lambda bi, i, j: (0, j))] + extra_specs,
        out_specs=pl.BlockSpec((None, tm, tn), lambda bi, i, j: (bi, i, j)),
        out_shape=jax.ShapeDtypeStruct((b, lt, n), out_dtype),
        scratch_shapes=[pltpu.VMEM((tm, d), BF16)],
        compiler_params=_params(3),
        name=name,
    )(x, mod, w, *extra)


def _dt_kernel(x_ref, mod_ref, w_ref, bias_ref, o_ref, u_ref, *, n_latent):
    i = pl.program_id(1)
    _modulate(x_ref, mod_ref, u_ref, i * x_ref.shape[0], n_latent)
    raw = _dot_nt(w_ref[...], u_ref[...])
    o_ref[...] = _softplus(raw + bias_ref[...])


def _dt_proj(x, mod, w_t, bias, n_latent):
    b, lt, d = x.shape
    nh = w_t.shape[0]
    tm = ROW_TILE
    return pl.pallas_call(
        functools.partial(_dt_kernel, n_latent=n_latent),
        grid=(b, lt // tm),
        in_specs=[pl.BlockSpec((None, tm, d), lambda bi, i: (bi, i, 0)),
                  pl.BlockSpec((None, 8, d), lambda bi, i: (bi, 0, 0)),
                  pl.BlockSpec((nh, d), lambda bi, i: (0, 0)),
                  pl.BlockSpec((nh, 1), lambda bi, i: (0, 0))],
        out_specs=pl.BlockSpec((None, nh, tm), lambda bi, i: (bi, 0, i)),
        out_shape=jax.ShapeDtypeStruct((b, nh, lt), F32),
        scratch_shapes=[pltpu.VMEM((tm, d), BF16)],
        compiler_params=_params(2),
        name="ssd_dt_proj",
    )(x, mod, w_t, bias)


def _conv_kernel(cur_ref, prev_ref, next_ref, w_ref, b_ref, o_ref, buf_ref, *, n_latent, n_rows):
    i = pl.program_id(1)
    tb = cur_ref.shape[0]
    taps = w_ref.shape[0]
    pad = taps // 2
    start = i * tb
    prev_ok = jnp.logical_and(start != 0, start != n_latent)
    next_ok = jnp.logical_and(start + tb != n_latent, start + tb != n_rows)
    buf_ref[0:8, :] = jnp.where(prev_ok, prev_ref[...].astype(F32), 0.0)
    buf_ref[8:8 + tb, :] = cur_ref[...].astype(F32)
    buf_ref[8 + tb:16 + tb, :] = jnp.where(next_ok, next_ref[...].astype(F32), 0.0)
    acc = b_ref[...] + w_ref[0:1, :] * buf_ref[8 - pad:8 - pad + tb, :]
    for k in range(1, taps):
        acc = acc + w_ref[k:k + 1, :] * buf_ref[8 - pad + k:8 - pad + k + tb, :]
    o_ref[...] = _silu(acc).astype(o_ref.dtype)


def _conv_silu(h, conv_w, conv_b, n_latent):
    b, lt, c = h.shape
    tb, cb = CONV_TILE, 1024 if c % 1024 == 0 else c
    r8 = tb // 8
    last8 = lt // 8 - 1
    return pl.pallas_call(
        functools.partial(_conv_kernel, n_latent=n_latent, n_rows=lt),
        grid=(b, lt // tb, c // cb),
        in_specs=[pl.BlockSpec((None, tb, cb), lambda bi, i, j: (bi, i, j)),
                  pl.BlockSpec((None, 8, cb), lambda bi, i, j: (bi, jnp.maximum(i * r8 - 1, 0), j)),
                  pl.BlockSpec((None, 8, cb), lambda bi, i, j: (bi, jnp.minimum((i + 1) * r8, last8), j)),
                  pl.BlockSpec((conv_w.shape[0], cb), lambda bi, i, j: (0, j)),
                  pl.BlockSpec((1, cb), lambda bi, i, j: (0, j))],
        out_specs=pl.BlockSpec((None, tb, cb), lambda bi, i, j: (bi, i, j)),
        out_shape=jax.ShapeDtypeStruct((b, lt, c), BF16),
        scratch_shapes=[pltpu.VMEM((tb + 16, cb), F32)],
        compiler_params=_params(3),
        name="ssd_conv_silu",
    )(h, h, h, conv_w, conv_b.reshape(1, c))


def _scan_kernel(x_ref, b_ref, c_ref, dt_ref, a_ref, *rest, reverse, finish):
    if finish:
        yf_ref, z_ref, dskip_ref, nw_ref, o_ref, h_ref = rest
    else:
        o_ref, h_ref = rest
    t = pl.program_id(2)
    q = SSD_CHUNK
    p = SSD_HEAD_DIM
    n_heads = x_ref.shape[1] // p
    n_chunks = x_ref.shape[0] // q

    @pl.when(t == 0)
    def _():
        h_ref[...] = jnp.zeros_like(h_ref)

    li = lax.broadcasted_iota(jnp.int32, (q, q), 0)
    ri = lax.broadcasted_iota(jnp.int32, (q, q), 1)
    keep = (ri >= li) if reverse else (ri <= li)
    keep_t = (li >= ri) if reverse else (li <= ri)
    cum_mat = jnp.where(keep_t, 1.0, 0.0).astype(BF16)
    lane = lax.broadcasted_iota(jnp.int32, (q, 2 * p), 1)
    first_half = lane < p
    lo_mask = jnp.where(first_half, 1.0, 0.0).astype(BF16)
    hi_mask = jnp.where(first_half, 0.0, 1.0).astype(BF16)
    sel = (lax.broadcasted_iota(jnp.int32, (n_heads, n_heads * p), 1) // p
           == lax.broadcasted_iota(jnp.int32, (n_heads, n_heads * p), 0))
    a_col = a_ref[...]

    order = range(n_chunks - 1, -1, -1) if reverse else range(n_chunks)
    for ck in order:
        rows = pl.ds(ck * q, q)
        xg = x_ref[rows, :]
        bg = b_ref[rows, :]
        cg = c_ref[rows, :]
        dt_t = dt_ref[:, rows]
        a_t = dt_t * a_col
        a_hi = a_t.astype(BF16)
        r1 = a_t - a_hi.astype(F32)
        a_mid = r1.astype(BF16)
        a_lo = (r1 - a_mid.astype(F32)).astype(BF16)
        cs_t = _dot(a_hi, cum_mat) + _dot(a_mid, cum_mat) + _dot(a_lo, cum_mat)
        tot = jnp.sum(a_t, axis=1, keepdims=True)
        w_t = dt_t * jnp.exp(tot - cs_t)
        decay_row = jnp.sum(jnp.where(sel, jnp.exp(tot), 0.0), axis=0, keepdims=True)

        cb = _dot_nt(cg, bg)
        bg_t = bg.astype(F32).T
        state = h_ref[...]
        y_state = _dot(cg, state.astype(BF16))

        y_parts, inc_parts = [], []
        for pr in range(n_heads // 2):
            lhs_top, lhs_bot, cs_cols = [], [], []
            for e in (2 * pr, 2 * pr + 1):
                cs_col = jnp.sum(jnp.where(keep, a_t[e:e + 1, :], 0.0), axis=1, keepdims=True)
                seg = jnp.where(keep, cs_col - cs_t[e:e + 1, :], -jnp.inf)
                m_e = jnp.exp(seg) * (cb * dt_t[e:e + 1, :])
                lhs_top.append(m_e.astype(BF16))
                lhs_bot.append((bg_t * w_t[e:e + 1, :]).astype(BF16))
                cs_cols.append(cs_col)
            lhs = jnp.concatenate([jnp.concatenate(lhs_top, axis=1),
                                   jnp.concatenate(lhs_bot, axis=1)], axis=0)
            x_pair = xg[:, 2 * pr * p:(2 * pr + 2) * p]
            rhs = jnp.concatenate([x_pair * lo_mask, x_pair * hi_mask], axis=0)
            out = _dot(lhs, rhs)
            from_start = jnp.exp(jnp.where(first_half, cs_cols[0], cs_cols[1]))
            y_parts.append(out[:q, :] + y_state[:, 2 * pr * p:(2 * pr + 2) * p] * from_start)
            inc_parts.append(out[q:, :])
        y = jnp.concatenate(y_parts, axis=1)
        h_ref[...] = state * decay_row + jnp.concatenate(inc_parts, axis=1)

        if finish:
            y = y + yf_ref[rows, :].astype(F32) + dskip_ref[...] * xg.astype(F32)
            y = y * z_ref[rows, :].astype(F32)
            ms = jnp.mean(y * y, axis=-1, keepdims=True)
            y = y * lax.rsqrt(ms + RMS_EPS) * nw_ref[...]
        o_ref[rows, :] = y.astype(o_ref.dtype)


def _ssd_scan(xbc, dt_t, a_cols, n_latent, *, reverse, finish_inputs=None):
    b, lt, _ = xbc.shape
    g, n, p = SSD_GROUPS, SSD_STATE, SSD_HEAD_DIM
    heads = dt_t.shape[1] // 2
    e = heads // g
    gw = e * p
    d_inner = heads * p
    tb = SCAN_TILE
    nblk = lt // tb
    n_lat_blk = n_latent // tb
    direction = 1 if reverse else 0

    def blk(t):
        n_ctx_blk = nblk - n_lat_blk
        if reverse:
            return jnp.where(t < n_ctx_blk, nblk - 1 - t, nblk - 1 - t)
        return jnp.where(t < n_ctx_blk, n_lat_blk + t, t - n_ctx_blk)

    in_specs = [pl.BlockSpec((None, tb, gw), lambda bi, gi, t: (bi, blk(t), gi)),
                pl.BlockSpec((None, tb, n), lambda bi, gi, t: (bi, blk(t), d_inner // n + gi)),
                pl.BlockSpec((None, tb, n), lambda bi, gi, t: (bi, blk(t), d_inner // n + g + gi)),
                pl.BlockSpec((None, e, tb), lambda bi, gi, t: (bi, direction * g + gi, blk(t))),
                pl.BlockSpec((e, 1), lambda bi, gi, t: (direction * g + gi, 0))]
    args = [xbc, xbc, xbc, dt_t, a_cols]
    if finish_inputs is not None:
        y_fwd, z_act, d_skip, norm_w = finish_inputs
        in_specs += [pl.BlockSpec((None, tb, gw), lambda bi, gi, t: (bi, blk(t), gi)),
                     pl.BlockSpec((None, tb, gw), lambda bi, gi, t: (bi, blk(t), gi)),
                     pl.BlockSpec((1, gw), lambda bi, gi, t: (0, gi)),
                     pl.BlockSpec((1, gw), lambda bi, gi, t: (0, gi))]
        args += [y_fwd, z_act, d_skip, norm_w]
    return pl.pallas_call(
        functools.partial(_scan_kernel, reverse=reverse, finish=finish_inputs is not None),
        grid=(b, g, nblk),
        in_specs=in_specs,
        out_specs=pl.BlockSpec((None, tb, gw), lambda bi, gi, t: (bi, blk(t), gi)),
        out_shape=jax.ShapeDtypeStruct((b, lt, d_inner), BF16),
        scratch_shapes=[pltpu.VMEM((n, gw), F32)],
        compiler_params=_params(3),
        name="ssd_scan_bwd" if reverse else "ssd_scan_fwd",
    )(*args)


def _outproj_kernel(y_ref, w_ref, x_ref, mod_ref, g_ref, b_ref, o_ref, acc_ref, *, n_latent, alpha):
    i, k = pl.program_id(1), pl.program_id(2)
    tm = x_ref.shape[0]

    @pl.when(k == 0)
    def _():
        acc_ref[...] = jnp.zeros_like(acc_ref)

    acc_ref[...] += _dot(y_ref[...], w_ref[...])

    @pl.when(k == pl.num_programs(2) - 1)
    def _():
        rows = i * tm + lax.broadcasted_iota(jnp.int32, (tm, 1), 0)
        gate = jnp.where(rows >= n_latent, mod_ref[2:3, :], mod_ref[5:6, :])
        r = alpha * x_ref[...] + gate * acc_ref[...]
        mu = jnp.mean(r, axis=-1, keepdims=True)
        rc = r - mu
        var = jnp.mean(rc * rc, axis=-1, keepdims=True)
        o_ref[...] = rc * lax.rsqrt(var + LN_EPS) * g_ref[...] + b_ref[...]


def _outproj_norm(y, w, x, mod, ln_g, ln_b, n_latent, alpha, *, rows):
    b, _, d = x.shape
    kdim = y.shape[2]
    tm = ROW_TILE if rows % ROW_TILE == 0 else LATENT_TILE
    tk = min(512, kdim)
    return pl.pallas_call(
        functools.partial(_outproj_kernel, n_latent=n_latent, alpha=alpha),
        grid=(b, rows // tm, kdim // tk),
        in_specs=[pl.BlockSpec((None, tm, tk), lambda bi, i, k: (bi, i, k)),
                  pl.BlockSpec((tk, d), lambda bi, i, k: (k, 0)),
                  pl.BlockSpec((None, tm, d), lambda bi, i, k: (bi, i, 0)),
                  pl.BlockSpec((None, 8, d), lambda bi, i, k: (bi, 0, 0)),
                  pl.BlockSpec((1, d), lambda bi, i, k: (0, 0)),
                  pl.BlockSpec((1, d), lambda bi, i, k: (0, 0))],
        out_specs=pl.BlockSpec((None, tm, d), lambda bi, i, k: (bi, i, 0)),
        out_shape=jax.ShapeDtypeStruct((b, rows, d), F32),
        scratch_shapes=[pltpu.VMEM((tm, d), F32)],
        compiler_params=_params(3),
        name="outproj_deepnorm",
    )(y, w, x, mod, ln_g.reshape(1, d), ln_b.reshape(1, d))


def _qup_kernel(cq_ref, w_ref, cos_ref, sin_ref, o_ref):
    acc = _dot(cq_ref[...], w_ref[...])
    hi = acc[:, QK_NOPE:]
    o_ref[:, :QK_NOPE] = acc[:, :QK_NOPE].astype(o_ref.dtype)
    o_ref[:, QK_NOPE:] = (hi * cos_ref[...] + pltpu.roll(hi, 64, 1) * sin_ref[...]).astype(o_ref.dtype)


def _q_up(cqkv, w_q, cos_t, sin_t):
    b, lt, _ = cqkv.shape
    h, r, wq = w_q.shape
    tm = ROW_TILE
    return pl.pallas_call(
        _qup_kernel,
        grid=(b, lt // tm, h),
        in_specs=[pl.BlockSpec((None, tm, r), lambda bi, i, hi: (bi, i, 0)),
                  pl.BlockSpec((None, r, wq), lambda bi, i, hi: (hi, 0, 0)),
                  pl.BlockSpec((tm, LANES), lambda bi, i, hi: (i, 0)),
                  pl.BlockSpec((tm, LANES), lambda bi, i, hi: (i, 0))],
        out_specs=pl.BlockSpec((None, None, tm, wq), lambda bi, i, hi: (bi, hi, i, 0)),
        out_shape=jax.ShapeDtypeStruct((b, h, lt, wq), BF16),
        compiler_params=_params(3),
        name="mla_q_up",
    )(cqkv, w_q, cos_t, sin_t)


def _kvup_kernel(ckv_ref, w_ref, kr_ref, k_ref, v_ref):
    acc = _dot(ckv_ref[...], w_ref[...])
    k_ref[:, :QK_NOPE] = acc[:, :QK_NOPE].astype(k_ref.dtype)
    k_ref[:, QK_NOPE:] = kr_ref[...]
    v_ref[...] = acc[:, QK_NOPE:].astype(v_ref.dtype)


def _kv_up(cqkv, w_kv, kr):
    b, lt, _ = cqkv.shape
    h, r, wkv = w_kv.shape
    tm = ROW_TILE
    return pl.pallas_call(
        _kvup_kernel,
        grid=(b, lt // tm, h),
        in_specs=[pl.BlockSpec((None, tm, r), lambda bi, i, hi: (bi, i, 1)),
                  pl.BlockSpec((None, r, wkv), lambda bi, i, hi: (hi, 0, 0)),
                  pl.BlockSpec((None, tm, LANES), lambda bi, i, hi: (bi, i, 0))],
        out_specs=[pl.BlockSpec((None, None, tm, 2 * LANES), lambda bi, i, hi: (bi, hi, i, 0)),
                   pl.BlockSpec((None, None, tm, V_DIM), lambda bi, i, hi: (bi, hi, i, 0))],
        out_shape=[jax.ShapeDtypeStruct((b, h, lt, 2 * LANES), BF16),
                   jax.ShapeDtypeStruct((b, h, lt, V_DIM), BF16)],
        compiler_params=_params(3),
        name="mla_kv_up",
    )(cqkv, w_kv, kr)


def _attn_kernel(q_ref, k_ref, v_ref, g_ref, o_ref, m_ref, l_ref, acc_ref, *, tk):
    n_k = k_ref.shape[0] // tk
    m_ref[...] = jnp.full_like(m_ref, -jnp.inf)
    l_ref[...] = jnp.zeros_like(l_ref)
    acc_ref[...] = jnp.zeros_like(acc_ref)
    q = q_ref[...]

    def body(t, carry):
        rows = pl.ds(pl.multiple_of(t * tk, tk), tk)
        s = _dot_nt(q, k_ref[rows, :])
        m_prev = m_ref[...]
        m_new = jnp.maximum(m_prev, jnp.max(s, axis=-1, keepdims=True))
        alpha = jnp.exp2(m_prev - m_new)
        pexp = jnp.exp2(s - m_new)
        l_ref[...] = alpha * l_ref[...] + jnp.sum(pexp, axis=-1, keepdims=True)
        acc_ref[...] = alpha * acc_ref[...] + _dot(pexp.astype(BF16), v_ref[rows, :])
        m_ref[...] = m_new
        return carry

    lax.fori_loop(0, n_k, body, 0)
    o_ref[...] = (acc_ref[...] / l_ref[...] * g_ref[...].astype(F32)).astype(o_ref.dtype)


def _attention_latent(qh, kh, vh, gate, n_latent):
    b, h, lt, wq = qh.shape
    tq, tk = ATTN_Q_TILE, ATTN_K_TILE
    return pl.pallas_call(
        functools.partial(_attn_kernel, tk=tk),
        grid=(b, h, n_latent // tq),
        in_specs=[pl.BlockSpec((None, None, tq, wq), lambda bi, hi, i: (bi, hi, i, 0)),
                  pl.BlockSpec((None, None, lt, wq), lambda bi, hi, i: (bi, hi, 0, 0)),
                  pl.BlockSpec((None, None, lt, V_DIM), lambda bi, hi, i: (bi, hi, 0, 0)),
                  pl.BlockSpec((None, tq, V_DIM), lambda bi, hi, i: (bi, i, hi))],
        out_specs=pl.BlockSpec((None, tq, V_DIM), lambda bi, hi, i: (bi, i, hi)),
        out_shape=jax.ShapeDtypeStruct((b, lt, h * V_DIM), BF16),
        scratch_shapes=[pltpu.VMEM((tq, 1), F32), pltpu.VMEM((tq, 1), F32), pltpu.VMEM((tq, V_DIM), F32)],
        compiler_params=_params(3),
        name="mla_attention",
    )(qh, kh, vh, gate)


def _attn_ctx_kernel(q_ref, k_ref, v_ref, g_ref, o_in_ref, o_ref):
    del o_in_ref
    s = _dot_nt(q_ref[...], k_ref[...])
    m = jnp.max(s, axis=-1, keepdims=True)
    pexp = jnp.exp2(s - m)
    l = jnp.sum(pexp, axis=-1, keepdims=True)
    o = _dot(pexp.astype(BF16), v_ref[...])
    o_ref[...] = (o / l * g_ref[...].astype(F32)).astype(o_ref.dtype)


def _attention_ctx(qh, kh, vh, gate, o, n_latent):
    b, h, lt, wq = qh.shape
    lc = lt - n_latent
    blk = n_latent // lc
    return pl.pallas_call(
        _attn_ctx_kernel,
        grid=(b, h),
        in_specs=[pl.BlockSpec((None, None, lc, wq), lambda bi, hi: (bi, hi, blk, 0)),
                  pl.BlockSpec((None, None, lc, wq), lambda bi, hi: (bi, hi, blk, 0)),
                  pl.BlockSpec((None, None, lc, V_DIM), lambda bi, hi: (bi, hi, blk, 0)),
                  pl.BlockSpec((None, lc, V_DIM), lambda bi, hi: (bi, blk, hi)),
                  pl.BlockSpec(memory_space=pl.ANY)],
        out_specs=pl.BlockSpec((None, lc, V_DIM), lambda bi, hi: (bi, blk, hi)),
        out_shape=jax.ShapeDtypeStruct(o.shape, o.dtype),
        input_output_aliases={4: 0},
        compiler_params=_params(2),
        name="mla_attention_ctx",
    )(qh, kh, vh, gate, o)


def _rope_tables(n_latent, n_ctx):
    t = jnp.arange(n_latent)
    rr = (t // GRID_W).astype(F32)
    cc = (t % GRID_W).astype(F32)
    n_freq = QK_ROPE // 4
    inv_freq = ROPE_BASE ** (-jnp.arange(n_freq, dtype=F32) / n_freq)
    ang = jnp.concatenate([rr[:, None] * inv_freq, cc[:, None] * inv_freq], -1)
    cos = jnp.repeat(jnp.cos(ang), 2, axis=-1)
    sin = jnp.repeat(jnp.sin(ang), 2, axis=-1)
    cos = jnp.concatenate([cos, jnp.ones((n_ctx, QK_ROPE), F32)], 0)
    sin = jnp.concatenate([sin, jnp.zeros((n_ctx, QK_ROPE), F32)], 0)
    pad = jnp.zeros((n_latent + n_ctx, LANES - QK_ROPE), F32)
    return jnp.concatenate([cos, pad], 1), jnp.concatenate([sin, pad], 1)


def _rotate_cols(w):
    wp = w.reshape(w.shape[:-1] + (w.shape[-1] // 2, 2))
    return jnp.stack([-wp[..., 1], wp[..., 0]], -1).reshape(w.shape)


def _ssd_layer(xs, mod, n_latent, w_in, conv_w, conv_b, dt_bias, a_log, d_skip, norm_w):
    heads = dt_bias.shape[-1]
    d_inner = heads * SSD_HEAD_DIM
    conv_ch = conv_w.shape[-1]
    w_z = w_in[:, :d_inner].astype(BF16)
    w_xbc = w_in[:, d_inner:d_inner + conv_ch].astype(BF16)
    w_dt_t = w_in[:, d_inner + conv_ch:].T.astype(BF16)
    z_act = _inproj(xs, mod, w_z, n_latent, tn=1024, epilogue="silu", name="ssd_in_z")
    xbc_pre = _inproj(xs, mod, w_xbc, n_latent, tn=1024, name="ssd_in_xbc")
    dt_t = _dt_proj(xs, mod, w_dt_t, dt_bias.reshape(2 * heads, 1), n_latent)
    xbc = _conv_silu(xbc_pre, conv_w, conv_b, n_latent)
    a_cols = (-jnp.exp(a_log.astype(F32))).reshape(2 * heads, 1)
    y_fwd = _ssd_scan(xbc, dt_t, a_cols, n_latent, reverse=False)
    d_row = jnp.repeat(d_skip, SSD_HEAD_DIM).reshape(1, d_inner)
    return _ssd_scan(xbc, dt_t, a_cols, n_latent, reverse=True,
                     finish_inputs=(y_fwd, z_act, d_row, norm_w.reshape(1, d_inner)))


def _mla_layer(xs, mod, n_latent, w_in, q_norm, kv_norm, w_uq, w_ukv, tables, need_ctx):
    q_rank, kv_rank = q_norm.shape[0], kv_norm.shape[0]
    h = MLA_HEADS
    cos_t, sin_t = tables
    w_lat = w_in[:, :q_rank + kv_rank].astype(BF16)
    w_kr = w_in[:, q_rank + kv_rank:q_rank + kv_rank + QK_ROPE]
    w_kr2 = jnp.concatenate([w_kr, _rotate_cols(w_kr)], 1).astype(BF16)
    w_gate = w_in[:, q_rank + kv_rank + QK_ROPE:].astype(BF16)
    norm_g = jnp.concatenate([q_norm, kv_norm]).reshape(1, -1).astype(F32)
    lat = _inproj(xs, mod, w_lat, n_latent, tn=q_rank, epilogue="rmsnorm", extra=(norm_g,),
                  name="mla_in_latents")
    kr = _inproj(xs, mod, w_kr2, n_latent, tn=LANES, epilogue="rope", extra=(cos_t, sin_t),
                 name="mla_in_kr")
    gate = _inproj(xs, mod, w_gate, n_latent, tn=1024, epilogue="silu", name="mla_in_gate")
    scale = (QK_NOPE + QK_ROPE) ** -0.5 * math.log2(math.e)
    wq = (w_uq * scale).reshape(q_rank, h, QK_NOPE + QK_ROPE).transpose(1, 0, 2)
    wq = jnp.concatenate([wq, _rotate_cols(wq[..., QK_NOPE:])], -1).astype(BF16)
    wkv = w_ukv.reshape(kv_rank, h, QK_NOPE + V_DIM).transpose(1, 0, 2).astype(BF16)
    qh = _q_up(lat, wq, cos_t, sin_t)
    kh, vh = _kv_up(lat, wkv, kr)
    o = _attention_latent(qh, kh, vh, gate, n_latent)
    if need_ctx:
        o = _attention_ctx(qh, kh, vh, gate, o, n_latent)
    return o


def kernel(x, c, ctx, c_ctx, ada_w, ada_b, ln_g, ln_b, ssd_w_in, ssd_conv_w, ssd_conv_b, ssd_dt_bias, ssd_a_log, ssd_d, ssd_norm_w, ssd_w_out, mla_w_in, mla_q_norm, mla_kv_norm, mla_w_uq, mla_w_ukv, mla_w_o):
    depth = ada_w.shape[0]
    b, n_latent, d = x.shape
    n_ctx = ctx.shape[1]
    alpha = (2 * depth) ** 0.25
    assert b + 1 <= 8

    c_rows = jnp.concatenate([c, c_ctx[None, :], jnp.zeros((8 - b - 1, d), F32)], 0)
    mods = _modulation(c_rows, ada_w, ada_b)
    tables = _rope_tables(n_latent, n_ctx)
    xs = jnp.concatenate([x, ctx], axis=1)

    for i in range(depth):
        last = i == depth - 1
        m = mods[i].reshape(8, 3, d)
        ctx_mod = jnp.broadcast_to(m[b][None], (b, 3, d))
        mod = jnp.concatenate([ctx_mod, m[:b], jnp.zeros((b, 2, d), F32)], 1)
        j = i // 2
        if i % 2 == 0:
            y = _ssd_layer(xs, mod, n_latent, ssd_w_in[j], ssd_conv_w[j], ssd_conv_b[j], ssd_dt_bias[j],
                           ssd_a_log[j], ssd_d[j], ssd_norm_w[j])
            w_out = ssd_w_out[j].astype(BF16)
        else:
            y = _mla_layer(xs, mod, n_latent, mla_w_in[j], mla_q_norm[j], mla_kv_norm[j], mla_w_uq[j],
                           mla_w_ukv[j], tables, need_ctx=not last)
            w_out = mla_w_o[j].astype(BF16)
        rows = n_latent if last else n_latent + n_ctx
        xs = _outproj_norm(y, w_out, xs, mod, ln_g[i], ln_b[i], n_latent, alpha, rows=rows)
    return xs
```

```python
import functools
import math

import jax
import jax.numpy as jnp
from jax import lax
from jax.experimental import pallas as pl
from jax.experimental.pallas import tpu as pltpu

F32 = jnp.float32
BF16 = jnp.bfloat16

GRID_W = 64
SSD_HEAD_DIM = 64
SSD_GROUPS = 8
SSD_STATE = 128
SSD_CHUNK = 128
MLA_HEADS = 16
QK_NOPE = 128
QK_ROPE = 64
V_DIM = 128
ROPE_BASE = 10000.0
LN_EPS = 1e-5
RMS_EPS = 1e-6

VMEM_LIMIT_BYTES = 56 * 1024 * 1024
LANES = 128

ROW_TILE = 768
LATENT_TILE = 512
CONV_TILE = 256
SCAN_TILE = 256
ATTN_Q_TILE = 1024
ATTN_K_TILE = 768
ATTN_SUB_ROWS = 256


def _params(n_axes):
    return pltpu.CompilerParams(dimension_semantics=("arbitrary",) * n_axes,
                                vmem_limit_bytes=VMEM_LIMIT_BYTES)


def _silu(v):
    return v * (1.0 / (1.0 + jnp.exp(-v)))


def _softplus(v):
    return jnp.maximum(v, 0.0) + jnp.log(1.0 + jnp.exp(-jnp.abs(v)))


def _dot(a, b):
    return jnp.dot(a, b, preferred_element_type=F32)


def _dot_nt(a, b):
    return lax.dot_general(a, b, (((1,), (1,)), ((), ())), preferred_element_type=F32)


def _mod_kernel(c_ref, w_ref, b_ref, o_ref):
    a = _silu(c_ref[...])
    o_ref[...] = jnp.dot(a, w_ref[...], preferred_element_type=F32,
                         precision=lax.Precision.HIGHEST) + b_ref[...]


def _modulation(c_rows, ada_w, ada_b):
    depth, d, n = ada_w.shape
    tn = 768 if n % 768 == 0 else n
    return pl.pallas_call(
        _mod_kernel,
        grid=(depth, n // tn),
        in_specs=[pl.BlockSpec((8, d), lambda i, j: (0, 0)),
                  pl.BlockSpec((None, d, tn), lambda i, j: (i, 0, j)),
                  pl.BlockSpec((None, 1, tn), lambda i, j: (i, 0, j))],
        out_specs=pl.BlockSpec((None, 8, tn), lambda i, j: (i, 0, j)),
        out_shape=jax.ShapeDtypeStruct((depth, 8, n), F32),
        compiler_params=_params(2),
        name="adaln_modulation",
    )(c_rows, ada_w, ada_b.reshape(depth, 1, n))


def _modulate(x_ref, mod_ref, u_ref, row0, n_latent):
    tm = x_ref.shape[0]
    rows = row0 + lax.broadcasted_iota(jnp.int32, (tm, 1), 0)
    is_ctx = rows >= n_latent
    shift = jnp.where(is_ctx, mod_ref[0:1, :], mod_ref[3:4, :])
    scale = jnp.where(is_ctx, mod_ref[1:2, :], mod_ref[4:5, :])
    u_ref[...] = (x_ref[...] * (1.0 + scale) + shift).astype(BF16)


def _inproj_kernel(x_ref, mod_ref, w_ref, *rest, n_latent, epilogue):
    o_ref, u_ref = rest[-2], rest[-1]
    extra = rest[:-2]
    i, j = pl.program_id(1), pl.program_id(2)
    tm = x_ref.shape[0]

    @pl.when(j == 0)
    def _():
        _modulate(x_ref, mod_ref, u_ref, i * tm, n_latent)

    acc = _dot(u_ref[...], w_ref[...])
    if epilogue == "silu":
        acc = _silu(acc)
    elif epilogue == "rmsnorm":
        g_ref, = extra
        ms = jnp.mean(acc * acc, axis=-1, keepdims=True)
        acc = acc * lax.rsqrt(ms + RMS_EPS) * g_ref[...]
    elif epilogue == "rope":
        cos_ref, sin_ref = extra
        acc = acc * cos_ref[...] + pltpu.roll(acc, 64, 1) * sin_ref[...]
    o_ref[...] = acc.astype(o_ref.dtype)


def _inproj(x, mod, w, n_latent, *, tn, epilogue="none", extra=(), out_dtype=BF16, name):
    b, lt, d = x.shape
    n = w.shape[1]
    tm = ROW_TILE
    tn = min(tn, n)
    extra_specs = []
    for e in extra:
        if e.shape[0] == 1:
            extra_specs.append(pl.BlockSpec((1, tn), lambda bi, i, j: (0, j)))
        else:
            extra_specs.append(pl.BlockSpec((tm, tn), lambda bi, i, j: (i, j)))
    return pl.pallas_call(
        functools.partial(_inproj_kernel, n_latent=n_latent, epilogue=epilogue),
        grid=(b, lt // tm, n // tn),
        in_specs=[pl.BlockSpec((None, tm, d), lambda bi, i, j: (bi, i, 0)),
                  pl.BlockSpec((None, 8, d), lambda bi, i, j: (bi, 0, 0)),
                  pl.BlockSpec((d, tn), lambda bi, i, j: (0, j))] + extra_specs,
        out_specs=pl.BlockSpec((None, tm, tn), lambda bi, i, j: (bi, i, j)),
        out_shape=jax.ShapeDtypeStruct((b, lt, n), out_dtype),
        scratch_shapes=[pltpu.VMEM((tm, d), BF16)],
        compiler_params=_params(3),
        name=name,
    )(x, mod, w, *extra)


def _dt_kernel(x_ref, mod_ref, w_ref, bias_ref, o_ref, u_ref, *, n_latent):
    i = pl.program_id(1)
    _modulate(x_ref, mod_ref, u_ref, i * x_ref.shape[0], n_latent)
    raw = _dot_nt(w_ref[...], u_ref[...])
    o_ref[...] = _softplus(raw + bias_ref[...])


def _dt_proj(x, mod, w_t, bias, n_latent):
    b, lt, d = x.shape
    nh = w_t.shape[0]
    tm = ROW_TILE
    return pl.pallas_call(
        functools.partial(_dt_kernel, n_latent=n_latent),
        grid=(b, lt // tm),
        in_specs=[pl.BlockSpec((None, tm, d), lambda bi, i: (bi, i, 0)),
                  pl.BlockSpec((None, 8, d), lambda bi, i: (bi, 0, 0)),
                  pl.BlockSpec((nh, d), lambda bi, i: (0, 0)),
                  pl.BlockSpec((nh, 1), lambda bi, i: (0, 0))],
        out_specs=pl.BlockSpec((None, nh, tm), lambda bi, i: (bi, 0, i)),
        out_shape=jax.ShapeDtypeStruct((b, nh, lt), F32),
        scratch_shapes=[pltpu.VMEM((tm, d), BF16)],
        compiler_params=_params(2),
        name="ssd_dt_proj",
    )(x, mod, w_t, bias)


def _conv_kernel(cur_ref, prev_ref, next_ref, w_ref, b_ref, o_ref, buf_ref, *, n_latent, n_rows):
    i = pl.program_id(1)
    tb = cur_ref.shape[0]
    taps = w_ref.shape[0]
    pad = taps // 2
    start = i * tb
    prev_ok = jnp.logical_and(start != 0, start != n_latent)
    next_ok = jnp.logical_and(start + tb != n_latent, start + tb != n_rows)
    buf_ref[0:8, :] = jnp.where(prev_ok, prev_ref[...].astype(F32), 0.0)
    buf_ref[8:8 + tb, :] = cur_ref[...].astype(F32)
    buf_ref[8 + tb:16 + tb, :] = jnp.where(next_ok, next_ref[...].astype(F32), 0.0)
    acc = b_ref[...] + w_ref[0:1, :] * buf_ref[8 - pad:8 - pad + tb, :]
    for k in range(1, taps):
        acc = acc + w_ref[k:k + 1, :] * buf_ref[8 - pad + k:8 - pad + k + tb, :]
    o_ref[...] = _silu(acc).astype(o_ref.dtype)


def _conv_silu(h, conv_w, conv_b, n_latent):
    b, lt, c = h.shape
    tb, cb = CONV_TILE, 1024 if c % 1024 == 0 else c
    r8 = tb // 8
    last8 = lt // 8 - 1
    return pl.pallas_call(
        functools.partial(_conv_kernel, n_latent=n_latent, n_rows=lt),
        grid=(b, lt // tb, c // cb),
        in_specs=[pl.BlockSpec((None, tb, cb), lambda bi, i, j: (bi, i, j)),
                  pl.BlockSpec((None, 8, cb), lambda bi, i, j: (bi, jnp.maximum(i * r8 - 1, 0), j)),
                  pl.BlockSpec((None, 8, cb), lambda bi, i, j: (bi, jnp.minimum((i + 1) * r8, last8), j)),
                  pl.BlockSpec((conv_w.shape[0], cb), lambda bi, i, j: (0, j)),
                  pl.BlockSpec((1, cb), lambda bi, i, j: (0, j))],
        out_specs=pl.BlockSpec((None, tb, cb), lambda bi, i, j: (bi, i, j)),
        out_shape=jax.ShapeDtypeStruct((b, lt, c), BF16),
        scratch_shapes=[pltpu.VMEM((tb + 16, cb), F32)],
        compiler_params=_params(3),
        name="ssd_conv_silu",
    )(h, h, h, conv_w, conv_b.reshape(1, c))


def _scan_kernel(x_ref, b_ref, c_ref, dt_ref, a_ref, *rest, reverse, finish):
    if finish:
        yf_ref, z_ref, dskip_ref, nw_ref, o_ref, h_ref = rest
    else:
        o_ref, h_ref = rest
    t = pl.program_id(2)
    q = SSD_CHUNK
    p = SSD_HEAD_DIM
    n_heads = x_ref.shape[1] // p
    n_chunks = x_ref.shape[0] // q

    @pl.when(t == 0)
    def _():
        h_ref[...] = jnp.zeros_like(h_ref)

    li = lax.broadcasted_iota(jnp.int32, (q, q), 0)
    ri = lax.broadcasted_iota(jnp.int32, (q, q), 1)
    keep = (ri >= li) if reverse else (ri <= li)
    keep_t = (li >= ri) if reverse else (li <= ri)
    cum_mat = jnp.where(keep_t, 1.0, 0.0).astype(BF16)
    lane = lax.broadcasted_iota(jnp.int32, (q, 2 * p), 1)
    first_half = lane < p
    lo_mask = jnp.where(first_half, 1.0, 0.0).astype(BF16)
    hi_mask = jnp.where(first_half, 0.0, 1.0).astype(BF16)
    sel = (lax.broadcasted_iota(jnp.int32, (n_heads, n_heads * p), 1) // p
           == lax.broadcasted_iota(jnp.int32, (n_heads, n_heads * p), 0))
    a_col = a_ref[...]

    order = range(n_chunks - 1, -1, -1) if reverse else range(n_chunks)
    for ck in order:
        rows = pl.ds(ck * q, q)
        xg = x_ref[rows, :]
        bg = b_ref[rows, :]
        cg = c_ref[rows, :]
        dt_t = dt_ref[:, rows]
        a_t = dt_t * a_col
        a_hi = a_t.astype(BF16)
        r1 = a_t - a_hi.astype(F32)
        a_mid = r1.astype(BF16)
        a_lo = (r1 - a_mid.astype(F32)).astype(BF16)
        cs_t = _dot(a_hi, cum_mat) + _dot(a_mid, cum_mat) + _dot(a_lo, cum_mat)
        tot = jnp.sum(a_t, axis=1, keepdims=True)
        w_t = dt_t * jnp.exp(tot - cs_t)
        decay_row = jnp.sum(jnp.where(sel, jnp.exp(tot), 0.0), axis=0, keepdims=True)

        cb = _dot_nt(cg, bg)
        bg_t = bg.astype(F32).T
        state = h_ref[...]
        y_state = _dot(cg, state.astype(BF16))

        y_parts, inc_parts = [], []
        for pr in range(n_heads // 2):
            lhs_top, lhs_bot, cs_cols = [], [], []
            for e in (2 * pr, 2 * pr + 1):
                cs_col = jnp.sum(jnp.where(keep, a_t[e:e + 1, :], 0.0), axis=1, keepdims=True)
                seg = jnp.where(keep, cs_col - cs_t[e:e + 1, :], -jnp.inf)
                m_e = jnp.exp(seg) * (cb * dt_t[e:e + 1, :])
                lhs_top.append(m_e.astype(BF16))
                lhs_bot.append((bg_t * w_t[e:e + 1, :]).astype(BF16))
                cs_cols.append(cs_col)
            lhs = jnp.concatenate([jnp.concatenate(lhs_top, axis=1),
                                   jnp.concatenate(lhs_bot, axis=1)], axis=0)
            x_pair = xg[:, 2 * pr * p:(2 * pr + 2) * p]
            rhs = jnp.concatenate([x_pair * lo_mask, x_pair * hi_mask], axis=0)
            out = _dot(lhs, rhs)
            from_start = jnp.exp(jnp.where(first_half, cs_cols[0], cs_cols[1]))
            y_parts.append(out[:q, :] + y_state[:, 2 * pr * p:(2 * pr + 2) * p] * from_start)
            inc_parts.append(out[q:, :])
        y = jnp.concatenate(y_parts, axis=1)
        h_ref[...] = state * decay_row + jnp.concatenate(inc_parts, axis=1)

        if finish:
            y = y + yf_ref[rows, :].astype(F32) + dskip_ref[...] * xg.astype(F32)
            y = y * z_ref[rows, :].astype(F32)
            ms = jnp.mean(y * y, axis=-1, keepdims=True)
            y = y * lax.rsqrt(ms + RMS_EPS) * nw_ref[...]
        o_ref[rows, :] = y.astype(o_ref.dtype)


def _ssd_scan(xbc, dt_t, a_cols, n_latent, *, reverse, finish_inputs=None):
    b, lt, _ = xbc.shape
    g, n, p = SSD_GROUPS, SSD_STATE, SSD_HEAD_DIM
    heads = dt_t.shape[1] // 2
    e = heads // g
    gw = e * p
    d_inner = heads * p
    tb = SCAN_TILE
    nblk = lt // tb
    n_lat_blk = n_latent // tb
    direction = 1 if reverse else 0

    def blk(t):
        n_ctx_blk = nblk - n_lat_blk
        if reverse:
            return jnp.where(t < n_ctx_blk, nblk - 1 - t, nblk - 1 - t)
        return jnp.where(t < n_ctx_blk, n_lat_blk + t, t - n_ctx_blk)

    in_specs = [pl.BlockSpec((None, tb, gw), lambda bi, gi, t: (bi, blk(t), gi)),
                pl.BlockSpec((None, tb, n), lambda bi, gi, t: (bi, blk(t), d_inner // n + gi)),
                pl.BlockSpec((None, tb, n), lambda bi, gi, t: (bi, blk(t), d_inner // n + g + gi)),
                pl.BlockSpec((None, e, tb), lambda bi, gi, t: (bi, direction * g + gi, blk(t))),
                pl.BlockSpec((e, 1), lambda bi, gi, t: (direction * g + gi, 0))]
    args = [xbc, xbc, xbc, dt_t, a_cols]
    if finish_inputs is not None:
        y_fwd, z_act, d_skip, norm_w = finish_inputs
        in_specs += [pl.BlockSpec((None, tb, gw), lambda bi, gi, t: (bi, blk(t), gi)),
                     pl.BlockSpec((None, tb, gw), lambda bi, gi, t: (bi, blk(t), gi)),
                     pl.BlockSpec((1, gw), lambda bi, gi, t: (0, gi)),
                     pl.BlockSpec((1, gw), lambda bi, gi, t: (0, gi))]
        args += [y_fwd, z_act, d_skip, norm_w]
    return pl.pallas_call(
        functools.partial(_scan_kernel, reverse=reverse, finish=finish_inputs is not None),
        grid=(b, g, nblk),
        in_specs=in_specs,
        out_specs=pl.BlockSpec((None, tb, gw), lambda bi, gi, t: (bi, blk(t), gi)),
        out_shape=jax.ShapeDtypeStruct((b, lt, d_inner), BF16),
        scratch_shapes=[pltpu.VMEM((n, gw), F32)],
        compiler_params=_params(3),
        name="ssd_scan_bwd" if reverse else "ssd_scan_fwd",
    )(*args)


def _outproj_kernel(y_ref, w_ref, x_ref, mod_ref, g_ref, b_ref, o_ref, acc_ref, *, n_latent, alpha):
    i, k = pl.program_id(1), pl.program_id(2)
    tm = x_ref.shape[0]

    @pl.when(k == 0)
    def _():
        acc_ref[...] = jnp.zeros_like(acc_ref)

    acc_ref[...] += _dot(y_ref[...], w_ref[...])

    @pl.when(k == pl.num_programs(2) - 1)
    def _():
        rows = i * tm + lax.broadcasted_iota(jnp.int32, (tm, 1), 0)
        gate = jnp.where(rows >= n_latent, mod_ref[2:3, :], mod_ref[5:6, :])
        r = alpha * x_ref[...] + gate * acc_ref[...]
        mu = jnp.mean(r, axis=-1, keepdims=True)
        rc = r - mu
        var = jnp.mean(rc * rc, axis=-1, keepdims=True)
        o_ref[...] = rc * lax.rsqrt(var + LN_EPS) * g_ref[...] + b_ref[...]


def _outproj_norm(y, w, x, mod, ln_g, ln_b, n_latent, alpha, *, rows):
    b, _, d = x.shape
    kdim = y.shape[2]
    tm = ROW_TILE if rows % ROW_TILE == 0 else LATENT_TILE
    tk = min(512, kdim)
    return pl.pallas_call(
        functools.partial(_outproj_kernel, n_latent=n_latent, alpha=alpha),
        grid=(b, rows // tm, kdim // tk),
        in_specs=[pl.BlockSpec((None, tm, tk), lambda bi, i, k: (bi, i, k)),
                  pl.BlockSpec((tk, d), lambda bi, i, k: (k, 0)),
                  pl.BlockSpec((None, tm, d), lambda bi, i, k: (bi, i, 0)),
                  pl.BlockSpec((None, 8, d), lambda bi, i, k: (bi, 0, 0)),
                  pl.BlockSpec((1, d), lambda bi, i, k: (0, 0)),
                  pl.BlockSpec((1, d), lambda bi, i, k: (0, 0))],
        out_specs=pl.BlockSpec((None, tm, d), lambda bi, i, k: (bi, i, 0)),
        out_shape=jax.ShapeDtypeStruct((b, rows, d), F32),
        scratch_shapes=[pltpu.VMEM((tm, d), F32)],
        compiler_params=_params(3),
        name="outproj_deepnorm",
    )(y, w, x, mod, ln_g.reshape(1, d), ln_b.reshape(1, d))


def _qup_kernel(cq_ref, w_ref, cos_ref, sin_ref, o_ref):
    acc = _dot(cq_ref[...], w_ref[...])
    hi = acc[:, QK_NOPE:]
    o_ref[:, :QK_NOPE] = acc[:, :QK_NOPE].astype(o_ref.dtype)
    o_ref[:, QK_NOPE:] = (hi * cos_ref[...] + pltpu.roll(hi, 64, 1) * sin_ref[...]).astype(o_ref.dtype)


def _q_up(cqkv, w_q, cos_t, sin_t):
    b, lt, _ = cqkv.shape
    h, r, wq = w_q.shape
    tm = ROW_TILE
    return pl.pallas_call(
        _qup_kernel,
        grid=(b, lt // tm, h),
        in_specs=[pl.BlockSpec((None, tm, r), lambda bi, i, hi: (bi, i, 0)),
                  pl.BlockSpec((None, r, wq), lambda bi, i, hi: (hi, 0, 0)),
                  pl.BlockSpec((tm, LANES), lambda bi, i, hi: (i, 0)),
                  pl.BlockSpec((tm, LANES), lambda bi, i, hi: (i, 0))],
        out_specs=pl.BlockSpec((None, None, tm, wq), lambda bi, i, hi: (bi, hi, i, 0)),
        out_shape=jax.ShapeDtypeStruct((b, h, lt, wq), BF16),
        compiler_params=_params(3),
        name="mla_q_up",
    )(cqkv, w_q, cos_t, sin_t)


def _kvup_kernel(ckv_ref, w_ref, kr_ref, k_ref, v_ref):
    acc = _dot(ckv_ref[...], w_ref[...])
    k_ref[:, :QK_NOPE] = acc[:, :QK_NOPE].astype(k_ref.dtype)
    k_ref[:, QK_NOPE:] = kr_ref[...]
    v_ref[...] = acc[:, QK_NOPE:].astype(v_ref.dtype)


def _kv_up(cqkv, w_kv, kr):
    b, lt, _ = cqkv.shape
    h, r, wkv = w_kv.shape
    tm = ROW_TILE
    return pl.pallas_call(
        _kvup_kernel,
        grid=(b, lt // tm, h),
        in_specs=[pl.BlockSpec((None, tm, r), lambda bi, i, hi: (bi, i, 1)),
                  pl.BlockSpec((None, r, wkv), lambda bi, i, hi: (hi, 0, 0)),
                  pl.BlockSpec((None, tm, LANES), lambda bi, i, hi: (bi, i, 0))],
        out_specs=[pl.BlockSpec((None, None, tm, 2 * LANES), lambda bi, i, hi: (bi, hi, i, 0)),
                   pl.BlockSpec((None, None, tm, V_DIM), lambda bi, i, hi: (bi, hi, i, 0))],
        out_shape=[jax.ShapeDtypeStruct((b, h, lt, 2 * LANES), BF16),
                   jax.ShapeDtypeStruct((b, h, lt, V_DIM), BF16)],
        compiler_params=_params(3),
        name="mla_kv_up",
    )(cqkv, w_kv, kr)


def _attn_kernel(q_ref, k_ref, v_ref, g_ref, o_ref, s_ref, m_ref, l_ref, acc_ref, *, tk, sub):
    tq = q_ref.shape[0]
    n_k = k_ref.shape[0] // tk
    n_grp = tk // LANES
    n_sub = tq // sub
    m_ref[...] = jnp.full_like(m_ref, -jnp.inf)
    l_ref[...] = jnp.zeros_like(l_ref)
    acc_ref[...] = jnp.zeros_like(acc_ref)

    def key_rows(t):
        return pl.ds(pl.multiple_of(t * tk, tk), tk)

    def scores(t, slot, i):
        r = slice(i * sub, (i + 1) * sub)
        s_ref[slot, r, :] = _dot_nt(q_ref[r, :], k_ref[key_rows(t), :])

    def update(t, slot, i):
        r = slice(i * sub, (i + 1) * sub)
        groups = [s_ref[slot, r, j * LANES:(j + 1) * LANES] for j in range(n_grp)]
        m_part = functools.reduce(jnp.maximum, groups)
        m_prev = m_ref[r, :]
        m_new = jnp.maximum(m_prev, jnp.max(m_part, axis=-1, keepdims=True))
        alpha = jnp.exp2(m_prev - m_new)
        p_groups = [jnp.exp2(sg - m_new) for sg in groups]
        l_ref[r, :] = alpha * l_ref[r, :] + functools.reduce(jnp.add, p_groups)
        pexp = jnp.concatenate(p_groups, axis=1).astype(BF16)
        acc_ref[r, :] = alpha * acc_ref[r, :] + _dot(pexp, v_ref[key_rows(t), :])
        m_ref[r, :] = m_new

    def step(t, slot):
        for i in range(n_sub):
            scores(t + 1, 1 - slot, i)
            update(t, slot, i)

    for i in range(n_sub):
        scores(0, 0, i)

    def body(u, carry):
        step(2 * u, 0)
        step(2 * u + 1, 1)
        return carry

    lax.fori_loop(0, (n_k - 1) // 2, body, 0)
    if (n_k - 1) % 2:
        step(n_k - 2, 0)
    for i in range(n_sub):
        update(n_k - 1, (n_k - 1) % 2, i)
    l = jnp.sum(l_ref[...], axis=-1, keepdims=True)
    o_ref[...] = (acc_ref[...] / l * g_ref[...].astype(F32)).astype(o_ref.dtype)


def _attention_latent(qh, kh, vh, gate, n_latent):
    b, h, lt, wq = qh.shape
    tq, tk = ATTN_Q_TILE, ATTN_K_TILE
    return pl.pallas_call(
        functools.partial(_attn_kernel, tk=tk, sub=min(ATTN_SUB_ROWS, tq)),
        grid=(b, h, n_latent // tq),
        in_specs=[pl.BlockSpec((None, None, tq, wq), lambda bi, hi, i: (bi, hi, i, 0)),
                  pl.BlockSpec((None, None, lt, wq), lambda bi, hi, i: (bi, hi, 0, 0)),
                  pl.BlockSpec((None, None, lt, V_DIM), lambda bi, hi, i: (bi, hi, 0, 0)),
                  pl.BlockSpec((None, tq, V_DIM), lambda bi, hi, i: (bi, i, hi))],
        out_specs=pl.BlockSpec((None, tq, V_DIM), lambda bi, hi, i: (bi, i, hi)),
        out_shape=jax.ShapeDtypeStruct((b, lt, h * V_DIM), BF16),
        scratch_shapes=[pltpu.VMEM((2, tq, tk), F32), pltpu.VMEM((tq, LANES), F32),
                        pltpu.VMEM((tq, LANES), F32), pltpu.VMEM((tq, V_DIM), F32)],
        compiler_params=_params(3),
        name="mla_attention",
    )(qh, kh, vh, gate)


def _attn_ctx_kernel(q_ref, k_ref, v_ref, g_ref, o_in_ref, o_ref):
    del o_in_ref
    s = _dot_nt(q_ref[...], k_ref[...])
    m = jnp.max(s, axis=-1, keepdims=True)
    pexp = jnp.exp2(s - m)
    l = jnp.sum(pexp, axis=-1, keepdims=True)
    o = _dot(pexp.astype(BF16), v_ref[...])
    o_ref[...] = (o / l * g_ref[...].astype(F32)).astype(o_ref.dtype)


def _attention_ctx(qh, kh, vh, gate, o, n_latent):
    b, h, lt, wq = qh.shape
    lc = lt - n_latent
    blk = n_latent // lc
    return pl.pallas_call(
        _attn_ctx_kernel,
        grid=(b, h),
        in_specs=[pl.BlockSpec((None, None, lc, wq), lambda bi, hi: (bi, hi, blk, 0)),
                  pl.BlockSpec((None, None, lc, wq), lambda bi, hi: (bi, hi, blk, 0)),
                  pl.BlockSpec((None, None, lc, V_DIM), lambda bi, hi: (bi, hi, blk, 0)),
                  pl.BlockSpec((None, lc, V_DIM), lambda bi, hi: (bi, blk, hi)),
                  pl.BlockSpec(memory_space=pl.ANY)],
        out_specs=pl.BlockSpec((None, lc, V_DIM), lambda bi, hi: (bi, blk, hi)),
        out_shape=jax.ShapeDtypeStruct(o.shape, o.dtype),
        input_output_aliases={4: 0},
        compiler_params=_params(2),
        name="mla_attention_ctx",
    )(qh, kh, vh, gate, o)


def _rope_tables(n_latent, n_ctx):
    t = jnp.arange(n_latent)
    rr = (t // GRID_W).astype(F32)
    cc = (t % GRID_W).astype(F32)
    n_freq = QK_ROPE // 4
    inv_freq = ROPE_BASE ** (-jnp.arange(n_freq, dtype=F32) / n_freq)
    ang = jnp.concatenate([rr[:, None] * inv_freq, cc[:, None] * inv_freq], -1)
    cos = jnp.repeat(jnp.cos(ang), 2, axis=-1)
    sin = jnp.repeat(jnp.sin(ang), 2, axis=-1)
    cos = jnp.concatenate([cos, jnp.ones((n_ctx, QK_ROPE), F32)], 0)
    sin = jnp.concatenate([sin, jnp.zeros((n_ctx, QK_ROPE), F32)], 0)
    pad = jnp.zeros((n_latent + n_ctx, LANES - QK_ROPE), F32)
    return jnp.concatenate([cos, pad], 1), jnp.concatenate([sin, pad], 1)


def _rotate_cols(w):
    wp = w.reshape(w.shape[:-1] + (w.shape[-1] // 2, 2))
    return jnp.stack([-wp[..., 1], wp[..., 0]], -1).reshape(w.shape)


def _ssd_layer(xs, mod, n_latent, w_in, conv_w, conv_b, dt_bias, a_log, d_skip, norm_w):
    heads = dt_bias.shape[-1]
    d_inner = heads * SSD_HEAD_DIM
    conv_ch = conv_w.shape[-1]
    w_z = w_in[:, :d_inner].astype(BF16)
    w_xbc = w_in[:, d_inner:d_inner + conv_ch].astype(BF16)
    w_dt_t = w_in[:, d_inner + conv_ch:].T.astype(BF16)
    z_act = _inproj(xs, mod, w_z, n_latent, tn=1024, epilogue="silu", name="ssd_in_z")
    xbc_pre = _inproj(xs, mod, w_xbc, n_latent, tn=1024, name="ssd_in_xbc")
    dt_t = _dt_proj(xs, mod, w_dt_t, dt_bias.reshape(2 * heads, 1), n_latent)
    xbc = _conv_silu(xbc_pre, conv_w, conv_b, n_latent)
    a_cols = (-jnp.exp(a_log.astype(F32))).reshape(2 * heads, 1)
    y_fwd = _ssd_scan(xbc, dt_t, a_cols, n_latent, reverse=False)
    d_row = jnp.repeat(d_skip, SSD_HEAD_DIM).reshape(1, d_inner)
    return _ssd_scan(xbc, dt_t, a_cols, n_latent, reverse=True,
                     finish_inputs=(y_fwd, z_act, d_row, norm_w.reshape(1, d_inner)))


def _mla_layer(xs, mod, n_latent, w_in, q_norm, kv_norm, w_uq, w_ukv, tables, need_ctx):
    q_rank, kv_rank = q_norm.shape[0], kv_norm.shape[0]
    h = MLA_HEADS
    cos_t, sin_t = tables
    w_lat = w_in[:, :q_rank + kv_rank].astype(BF16)
    w_kr = w_in[:, q_rank + kv_rank:q_rank + kv_rank + QK_ROPE]
    w_kr2 = jnp.concatenate([w_kr, _rotate_cols(w_kr)], 1).astype(BF16)
    w_gate = w_in[:, q_rank + kv_rank + QK_ROPE:].astype(BF16)
    norm_g = jnp.concatenate([q_norm, kv_norm]).reshape(1, -1).astype(F32)
    lat = _inproj(xs, mod, w_lat, n_latent, tn=q_rank, epilogue="rmsnorm", extra=(norm_g,),
                  name="mla_in_latents")
    kr = _inproj(xs, mod, w_kr2, n_latent, tn=LANES, epilogue="rope", extra=(cos_t, sin_t),
                 name="mla_in_kr")
    gate = _inproj(xs, mod, w_gate, n_latent, tn=1024, epilogue="silu", name="mla_in_gate")
    scale = (QK_NOPE + QK_ROPE) ** -0.5 * math.log2(math.e)
    wq = (w_uq * scale).reshape(q_rank, h, QK_NOPE + QK_ROPE).transpose(1, 0, 2)
    wq = jnp.concatenate([wq, _rotate_cols(wq[..., QK_NOPE:])], -1).astype(BF16)
    wkv = w_ukv.reshape(kv_rank, h, QK_NOPE + V_DIM).transpose(1, 0, 2).astype(BF16)
    qh = _q_up(lat, wq, cos_t, sin_t)
    kh, vh = _kv_up(lat, wkv, kr)
    o = _attention_latent(qh, kh, vh, gate, n_latent)
    if need_ctx:
        o = _attention_ctx(qh, kh, vh, gate, o, n_latent)
    return o


def kernel(x, c, ctx, c_ctx, ada_w, ada_b, ln_g, ln_b, ssd_w_in, ssd_conv_w, ssd_conv_b, ssd_dt_bias, ssd_a_log, ssd_d, ssd_norm_w, ssd_w_out, mla_w_in, mla_q_norm, mla_kv_norm, mla_w_uq, mla_w_ukv, mla_w_o):
    depth = ada_w.shape[0]
    b, n_latent, d = x.shape
    n_ctx = ctx.shape[1]
    alpha = (2 * depth) ** 0.25
    assert b + 1 <= 8

    c_rows = jnp.concatenate([c, c_ctx[None, :], jnp.zeros((8 - b - 1, d), F32)], 0)
    mods = _modulation(c_rows, ada_w, ada_b)
    tables = _rope_tables(n_latent, n_ctx)
    xs = jnp.concatenate([x, ctx], axis=1)

    for i in range(depth):
        last = i == depth - 1
        m = mods[i].reshape(8, 3, d)
        ctx_mod = jnp.broadcast_to(m[b][None], (b, 3, d))
        mod = jnp.concatenate([ctx_mod, m[:b], jnp.zeros((b, 2, d), F32)], 1)
        j = i // 2
        if i % 2 == 0:
            y = _ssd_layer(xs, mod, n_latent, ssd_w_in[j], ssd_conv_w[j], ssd_conv_b[j], ssd_dt_bias[j],
                           ssd_a_log[j], ssd_d[j], ssd_norm_w[j])
            w_out = ssd_w_out[j].astype(BF16)
        else:
            y = _mla_layer(xs, mod, n_latent, mla_w_in[j], mla_q_norm[j], mla_kv_norm[j], mla_w_uq[j],
                           mla_w_ukv[j], tables, need_ctx=not last)
            w_out = mla_w_o[j].astype(BF16)
        rows = n_latent if last else n_latent + n_ctx
        xs = _outproj_norm(y, w_out, xs, mod, ln_g[i], ln_b[i], n_latent, alpha, rows=rows)
    return xs
```

```python
import functools
import math

import jax
import jax.numpy as jnp
from jax import lax
from jax.experimental import pallas as pl
from jax.experimental.pallas import tpu as pltpu

F32 = jnp.float32
BF16 = jnp.bfloat16

GRID_W = 64
SSD_HEAD_DIM = 64
SSD_GROUPS = 8
SSD_STATE = 128
SSD_CHUNK = 128
MLA_HEADS = 16
QK_NOPE = 128
QK_ROPE = 64
V_DIM = 128
ROPE_BASE = 10000.0
LN_EPS = 1e-5
RMS_EPS = 1e-6

VMEM_LIMIT_BYTES = 56 * 1024 * 1024
LANES = 128

ROW_TILE = 768
LATENT_TILE = 512
CONV_TILE = 256
CONV_CH_TILE = 2048
SCAN_TILE = 256
SCAN_GROUPS = 2
UP_HEADS_PER_STEP = 4
ATTN_Q_TILE = 1024
ATTN_K_TILE = 768
ATTN_SUB_ROWS = 256
ATTN_CHUNK_ROWS = 32


def _params(n_axes):
    return pltpu.CompilerParams(dimension_semantics=("arbitrary",) * n_axes,
                                vmem_limit_bytes=VMEM_LIMIT_BYTES)


def _silu(v):
    return v * (1.0 / (1.0 + jnp.exp(-v)))


def _softplus(v):
    return jnp.maximum(v, 0.0) + jnp.log(1.0 + jnp.exp(-jnp.abs(v)))


def _dot(a, b):
    return jnp.dot(a, b, preferred_element_type=F32)


def _dot_nt(a, b):
    return lax.dot_general(a, b, (((1,), (1,)), ((), ())), preferred_element_type=F32)


def _mod_kernel(c_ref, w_ref, b_ref, o_ref):
    a = _silu(c_ref[...])
    o_ref[...] = jnp.dot(a, w_ref[...], preferred_element_type=F32,
                         precision=lax.Precision.HIGHEST) + b_ref[...]


def _modulation(c_rows, ada_w, ada_b):
    depth, d, n = ada_w.shape
    tn = 768 if n % 768 == 0 else n
    return pl.pallas_call(
        _mod_kernel,
        grid=(depth, n // tn),
        in_specs=[pl.BlockSpec((8, d), lambda i, j: (0, 0)),
                  pl.BlockSpec((None, d, tn), lambda i, j: (i, 0, j)),
                  pl.BlockSpec((None, 1, tn), lambda i, j: (i, 0, j))],
        out_specs=pl.BlockSpec((None, 8, tn), lambda i, j: (i, 0, j)),
        out_shape=jax.ShapeDtypeStruct((depth, 8, n), F32),
        compiler_params=_params(2),
        name="adaln_modulation",
    )(c_rows, ada_w, ada_b.reshape(depth, 1, n))


def _modulate(x_ref, mod_ref, u_ref, row0, n_latent):
    tm = x_ref.shape[0]
    rows = row0 + lax.broadcasted_iota(jnp.int32, (tm, 1), 0)
    is_ctx = rows >= n_latent
    shift = jnp.where(is_ctx, mod_ref[0:1, :], mod_ref[3:4, :])
    scale = jnp.where(is_ctx, mod_ref[1:2, :], mod_ref[4:5, :])
    u_ref[...] = (x_ref[...] * (1.0 + scale) + shift).astype(BF16)


def _inproj_kernel(x_ref, mod_ref, w_ref, *rest, n_latent, epilogue):
    o_ref, u_ref = rest[-2], rest[-1]
    extra = rest[:-2]
    i, j = pl.program_id(1), pl.program_id(2)
    tm = x_ref.shape[0]

    @pl.when(j == 0)
    def _():
        _modulate(x_ref, mod_ref, u_ref, i * tm, n_latent)

    acc = _dot(u_ref[...], w_ref[...])
    if epilogue == "silu":
        acc = _silu(acc)
    elif epilogue == "rmsnorm":
        g_ref, = extra
        ms = jnp.mean(acc * acc, axis=-1, keepdims=True)
        acc = acc * lax.rsqrt(ms + RMS_EPS) * g_ref[...]
    elif epilogue == "rope":
        cos_ref, sin_ref = extra
        acc = acc * cos_ref[...] + pltpu.roll(acc, 64, 1) * sin_ref[...]
    o_ref[...] = acc.astype(o_ref.dtype)


def _inproj(x, mod, w, n_latent, *, tn, epilogue="none", extra=(), out_dtype=BF16, name):
    b, lt, d = x.shape
    n = w.shape[1]
    tm = ROW_TILE
    tn = min(tn, n)
    extra_specs = []
    for e in extra:
        if e.shape[0] == 1:
            extra_specs.append(pl.BlockSpec((1, tn), lambda bi, i, j: (0, j)))
        else:
            extra_specs.append(pl.BlockSpec((tm, tn), lambda bi, i, j: (i, j)))
    return pl.pallas_call(
        functools.partial(_inproj_kernel, n_latent=n_latent, epilogue=epilogue),
        grid=(b, lt // tm, n // tn),
        in_specs=[pl.BlockSpec((None, tm, d), lambda bi, i, j: (bi, i, 0)),
                  pl.BlockSpec((None, 8, d), lambda bi, i, j: (bi, 0, 0)),
                  pl.BlockSpec((d, tn), lambda bi, i, j: (0, j))] + extra_specs,
        out_specs=pl.BlockSpec((None, tm, tn), lambda bi, i, j: (bi, i, j)),
        out_shape=jax.ShapeDtypeStruct((b, lt, n), out_dtype),
        scratch_shapes=[pltpu.VMEM((tm, d), BF16)],
        compiler_params=_params(3),
        name=name,
    )(x, mod, w, *extra)


def _dt_kernel(x_ref, mod_ref, w_ref, bias_ref, o_ref, u_ref, *, n_latent):
    i = pl.program_id(1)
    _modulate(x_ref, mod_ref, u_ref, i * x_ref.shape[0], n_latent)
    raw = _dot_nt(w_ref[...], u_ref[...])
    o_ref[...] = _softplus(raw + bias_ref[...])


def _dt_proj(x, mod, w_t, bias, n_latent):
    b, lt, d = x.shape
    nh = w_t.shape[0]
    tm = ROW_TILE
    return pl.pallas_call(
        functools.partial(_dt_kernel, n_latent=n_latent),
        grid=(b, lt // tm),
        in_specs=[pl.BlockSpec((None, tm, d), lambda bi, i: (bi, i, 0)),
                  pl.BlockSpec((None, 8, d), lambda bi, i: (bi, 0, 0)),
                  pl.BlockSpec((nh, d), lambda bi, i: (0, 0)),
                  pl.BlockSpec((nh, 1), lambda bi, i: (0, 0))],
        out_specs=pl.BlockSpec((None, nh, tm), lambda bi, i: (bi, 0, i)),
        out_shape=jax.ShapeDtypeStruct((b, nh, lt), F32),
        scratch_shapes=[pltpu.VMEM((tm, d), BF16)],
        compiler_params=_params(2),
        name="ssd_dt_proj",
    )(x, mod, w_t, bias)


def _conv_kernel(cur_ref, prev_ref, next_ref, w_ref, b_ref, o_ref, buf_ref, *, n_latent, n_rows):
    i = pl.program_id(1)
    tb = cur_ref.shape[0]
    taps = w_ref.shape[0]
    pad = taps // 2
    start = i * tb
    prev_ok = jnp.logical_and(start != 0, start != n_latent)
    next_ok = jnp.logical_and(start + tb != n_latent, start + tb != n_rows)
    buf_ref[0:8, :] = jnp.where(prev_ok, prev_ref[...].astype(F32), 0.0)
    buf_ref[8:8 + tb, :] = cur_ref[...].astype(F32)
    buf_ref[8 + tb:16 + tb, :] = jnp.where(next_ok, next_ref[...].astype(F32), 0.0)
    acc = b_ref[...] + w_ref[0:1, :] * buf_ref[8 - pad:8 - pad + tb, :]
    for k in range(1, taps):
        acc = acc + w_ref[k:k + 1, :] * buf_ref[8 - pad + k:8 - pad + k + tb, :]
    o_ref[...] = _silu(acc).astype(o_ref.dtype)


def _conv_silu(h, conv_w, conv_b, n_latent):
    b, lt, c = h.shape
    tb, cb = CONV_TILE, CONV_CH_TILE if c % CONV_CH_TILE == 0 else c
    r8 = tb // 8
    last8 = lt // 8 - 1
    return pl.pallas_call(
        functools.partial(_conv_kernel, n_latent=n_latent, n_rows=lt),
        grid=(b, lt // tb, c // cb),
        in_specs=[pl.BlockSpec((None, tb, cb), lambda bi, i, j: (bi, i, j)),
                  pl.BlockSpec((None, 8, cb), lambda bi, i, j: (bi, jnp.maximum(i * r8 - 1, 0), j)),
                  pl.BlockSpec((None, 8, cb), lambda bi, i, j: (bi, jnp.minimum((i + 1) * r8, last8), j)),
                  pl.BlockSpec((conv_w.shape[0], cb), lambda bi, i, j: (0, j)),
                  pl.BlockSpec((1, cb), lambda bi, i, j: (0, j))],
        out_specs=pl.BlockSpec((None, tb, cb), lambda bi, i, j: (bi, i, j)),
        out_shape=jax.ShapeDtypeStruct((b, lt, c), BF16),
        scratch_shapes=[pltpu.VMEM((tb + 16, cb), F32)],
        compiler_params=_params(3),
        name="ssd_conv_silu",
    )(h, h, h, conv_w, conv_b.reshape(1, c))


def _scan_kernel(x_ref, b_ref, c_ref, dt_ref, a_ref, *rest, reverse, finish, n_groups):
    if finish:
        yf_ref, z_ref, dskip_ref, nw_ref, o_ref, h_ref = rest
    else:
        o_ref, h_ref = rest
    t = pl.program_id(2)
    q = SSD_CHUNK
    p = SSD_HEAD_DIM
    n = SSD_STATE
    gw = x_ref.shape[1] // n_groups
    n_heads = gw // p
    n_chunks = x_ref.shape[0] // q

    @pl.when(t == 0)
    def _():
        h_ref[...] = jnp.zeros_like(h_ref)

    li = lax.broadcasted_iota(jnp.int32, (q, q), 0)
    ri = lax.broadcasted_iota(jnp.int32, (q, q), 1)
    keep = (ri >= li) if reverse else (ri <= li)
    keep_t = (li >= ri) if reverse else (li <= ri)
    cum_mat = jnp.where(keep_t, 1.0, 0.0).astype(BF16)
    lane = lax.broadcasted_iota(jnp.int32, (q, 2 * p), 1)
    first_half = lane < p
    lo_mask = jnp.where(first_half, 1.0, 0.0).astype(BF16)
    hi_mask = jnp.where(first_half, 0.0, 1.0).astype(BF16)
    sel = (lax.broadcasted_iota(jnp.int32, (n_heads, gw), 1) // p
           == lax.broadcasted_iota(jnp.int32, (n_heads, gw), 0))

    order = range(n_chunks - 1, -1, -1) if reverse else range(n_chunks)
    for ck in order:
        rows = pl.ds(ck * q, q)
        for gi in range(n_groups):
            cols = slice(gi * gw, (gi + 1) * gw)
            hrows = slice(gi * n_heads, (gi + 1) * n_heads)
            xg = x_ref[rows, cols]
            bg = b_ref[rows, gi * n:(gi + 1) * n]
            cg = c_ref[rows, gi * n:(gi + 1) * n]
            dt_t = dt_ref[hrows, rows]
            a_t = dt_t * a_ref[hrows, :]
            a_hi = a_t.astype(BF16)
            r1 = a_t - a_hi.astype(F32)
            a_mid = r1.astype(BF16)
            a_lo = (r1 - a_mid.astype(F32)).astype(BF16)
            cs_t = _dot(a_hi, cum_mat) + _dot(a_mid, cum_mat) + _dot(a_lo, cum_mat)
            tot = jnp.sum(a_t, axis=1, keepdims=True)
            w_t = (dt_t * jnp.exp2(tot - cs_t)).astype(BF16)
            src_t = cs_t - jnp.log2(dt_t)
            decay_row = jnp.sum(jnp.where(sel, jnp.exp2(tot), 0.0), axis=0, keepdims=True)

            cb = _dot_nt(cg, bg).astype(BF16)
            bg_t = bg.astype(F32).T.astype(BF16)
            state = h_ref[gi]
            y_state = _dot(cg, state.astype(BF16))

            y_parts, inc_parts = [], []
            for pr in range(n_heads // 2):
                lhs_top, lhs_bot, cs_cols = [], [], []
                for e in (2 * pr, 2 * pr + 1):
                    cs_col = jnp.sum(jnp.where(keep, a_t[e:e + 1, :], 0.0), axis=1, keepdims=True)
                    seg = jnp.where(keep, cs_col - src_t[e:e + 1, :], -jnp.inf)
                    lhs_top.append(jnp.exp2(seg).astype(BF16) * cb)
                    lhs_bot.append(bg_t * w_t[e:e + 1, :])
                    cs_cols.append(cs_col)
                lhs = jnp.concatenate([jnp.concatenate(lhs_top, axis=1),
                                       jnp.concatenate(lhs_bot, axis=1)], axis=0)
                x_pair = xg[:, 2 * pr * p:(2 * pr + 2) * p]
                rhs = jnp.concatenate([x_pair * lo_mask, x_pair * hi_mask], axis=0)
                out = _dot(lhs, rhs)
                from_start = jnp.exp2(jnp.where(first_half, cs_cols[0], cs_cols[1]))
                y_parts.append(out[:q, :] + y_state[:, 2 * pr * p:(2 * pr + 2) * p] * from_start)
                inc_parts.append(out[q:, :])
            y = jnp.concatenate(y_parts, axis=1)
            h_ref[gi] = state * decay_row + jnp.concatenate(inc_parts, axis=1)

            if finish:
                y = y + yf_ref[rows, cols].astype(F32) + dskip_ref[:, cols] * xg.astype(F32)
                y = y * z_ref[rows, cols].astype(F32)
                ms = jnp.mean(y * y, axis=-1, keepdims=True)
                y = y * lax.rsqrt(ms + RMS_EPS) * nw_ref[:, cols]
            o_ref[rows, cols] = y.astype(o_ref.dtype)


def _ssd_scan(xbc, dt_t, a_cols, n_latent, *, reverse, finish_inputs=None):
    b, lt, _ = xbc.shape
    g, n, p = SSD_GROUPS, SSD_STATE, SSD_HEAD_DIM
    heads = dt_t.shape[1] // 2
    e = heads // g
    ng = SCAN_GROUPS
    gw = ng * e * p
    d_inner = heads * p
    tb = SCAN_TILE
    nblk = lt // tb
    n_lat_blk = n_latent // tb
    direction = 1 if reverse else 0
    gsteps = g // ng
    b_col0 = d_inner // (ng * n)

    def blk(t):
        if reverse:
            return nblk - 1 - t
        return jnp.where(t < nblk - n_lat_blk, n_lat_blk + t, t - (nblk - n_lat_blk))

    in_specs = [pl.BlockSpec((None, tb, gw), lambda bi, gi, t: (bi, blk(t), gi)),
                pl.BlockSpec((None, tb, ng * n), lambda bi, gi, t: (bi, blk(t), b_col0 + gi)),
                pl.BlockSpec((None, tb, ng * n), lambda bi, gi, t: (bi, blk(t), b_col0 + gsteps + gi)),
                pl.BlockSpec((None, ng * e, tb), lambda bi, gi, t: (bi, direction * gsteps + gi, blk(t))),
                pl.BlockSpec((ng * e, 1), lambda bi, gi, t: (direction * gsteps + gi, 0))]
    args = [xbc, xbc, xbc, dt_t, a_cols]
    if finish_inputs is not None:
        y_fwd, z_act, d_skip, norm_w = finish_inputs
        in_specs += [pl.BlockSpec((None, tb, gw), lambda bi, gi, t: (bi, blk(t), gi)),
                     pl.BlockSpec((None, tb, gw), lambda bi, gi, t: (bi, blk(t), gi)),
                     pl.BlockSpec((1, gw), lambda bi, gi, t: (0, gi)),
                     pl.BlockSpec((1, gw), lambda bi, gi, t: (0, gi))]
        args += [y_fwd, z_act, d_skip, norm_w]
    return pl.pallas_call(
        functools.partial(_scan_kernel, reverse=reverse, finish=finish_inputs is not None, n_groups=ng),
        grid=(b, gsteps, nblk),
        in_specs=in_specs,
        out_specs=pl.BlockSpec((None, tb, gw), lambda bi, gi, t: (bi, blk(t), gi)),
        out_shape=jax.ShapeDtypeStruct((b, lt, d_inner), BF16),
        scratch_shapes=[pltpu.VMEM((ng, n, e * p), F32)],
        compiler_params=_params(3),
        name="ssd_scan_bwd" if reverse else "ssd_scan_fwd",
    )(*args)


def _outproj_kernel(y_ref, w_ref, x_ref, mod_ref, g_ref, b_ref, o_ref, acc_ref, *, n_latent, alpha):
    i, k = pl.program_id(1), pl.program_id(2)
    tm = x_ref.shape[0]

    @pl.when(k == 0)
    def _():
        acc_ref[...] = jnp.zeros_like(acc_ref)

    acc_ref[...] += _dot(y_ref[...], w_ref[...])

    @pl.when(k == pl.num_programs(2) - 1)
    def _():
        rows = i * tm + lax.broadcasted_iota(jnp.int32, (tm, 1), 0)
        gate = jnp.where(rows >= n_latent, mod_ref[2:3, :], mod_ref[5:6, :])
        r = alpha * x_ref[...] + gate * acc_ref[...]
        mu = jnp.mean(r, axis=-1, keepdims=True)
        rc = r - mu
        var = jnp.mean(rc * rc, axis=-1, keepdims=True)
        o_ref[...] = rc * lax.rsqrt(var + LN_EPS) * g_ref[...] + b_ref[...]


def _outproj_norm(y, w, x, mod, ln_g, ln_b, n_latent, alpha, *, rows):
    b, _, d = x.shape
    kdim = y.shape[2]
    tm = ROW_TILE if rows % ROW_TILE == 0 else LATENT_TILE
    tk = min(512, kdim)
    return pl.pallas_call(
        functools.partial(_outproj_kernel, n_latent=n_latent, alpha=alpha),
        grid=(b, rows // tm, kdim // tk),
        in_specs=[pl.BlockSpec((None, tm, tk), lambda bi, i, k: (bi, i, k)),
                  pl.BlockSpec((tk, d), lambda bi, i, k: (k, 0)),
                  pl.BlockSpec((None, tm, d), lambda bi, i, k: (bi, i, 0)),
                  pl.BlockSpec((None, 8, d), lambda bi, i, k: (bi, 0, 0)),
                  pl.BlockSpec((1, d), lambda bi, i, k: (0, 0)),
                  pl.BlockSpec((1, d), lambda bi, i, k: (0, 0))],
        out_specs=pl.BlockSpec((None, tm, d), lambda bi, i, k: (bi, i, 0)),
        out_shape=jax.ShapeDtypeStruct((b, rows, d), F32),
        scratch_shapes=[pltpu.VMEM((tm, d), F32)],
        compiler_params=_params(3),
        name="outproj_deepnorm",
    )(y, w, x, mod, ln_g.reshape(1, d), ln_b.reshape(1, d))


def _qkvup_kernel(cq_ref, ckv_ref, wq_ref, wkv_ref, cos_ref, sin_ref, kr_ref, q_ref, k_ref, v_ref):
    cq = cq_ref[...]
    ckv = ckv_ref[...]
    for hh in range(wq_ref.shape[0]):
        acc = _dot(cq, wq_ref[hh])
        hi = acc[:, QK_NOPE:]
        q_ref[hh, :, :QK_NOPE] = acc[:, :QK_NOPE].astype(q_ref.dtype)
        q_ref[hh, :, QK_NOPE:] = (hi * cos_ref[...] + pltpu.roll(hi, 64, 1) * sin_ref[...]).astype(q_ref.dtype)
        acc = _dot(ckv, wkv_ref[hh])
        k_ref[hh, :, :QK_NOPE] = acc[:, :QK_NOPE].astype(k_ref.dtype)
        k_ref[hh, :, QK_NOPE:] = kr_ref[...]
        v_ref[hh] = acc[:, QK_NOPE:].astype(v_ref.dtype)


def _qkv_up(cqkv, w_q, w_kv, cos_t, sin_t, kr):
    b, lt, _ = cqkv.shape
    h, r, wq = w_q.shape
    tm = ROW_TILE
    hb = min(UP_HEADS_PER_STEP, h)
    return pl.pallas_call(
        _qkvup_kernel,
        grid=(b, lt // tm, h // hb),
        in_specs=[pl.BlockSpec((None, tm, r), lambda bi, i, hi: (bi, i, 0)),
                  pl.BlockSpec((None, tm, r), lambda bi, i, hi: (bi, i, 1)),
                  pl.BlockSpec((hb, r, wq), lambda bi, i, hi: (hi, 0, 0)),
                  pl.BlockSpec((hb, r, wq), lambda bi, i, hi: (hi, 0, 0)),
                  pl.BlockSpec((tm, LANES), lambda bi, i, hi: (i, 0)),
                  pl.BlockSpec((tm, LANES), lambda bi, i, hi: (i, 0)),
                  pl.BlockSpec((None, tm, LANES), lambda bi, i, hi: (bi, i, 0))],
        out_specs=[pl.BlockSpec((None, hb, tm, wq), lambda bi, i, hi: (bi, hi, i, 0)),
                   pl.BlockSpec((None, hb, tm, 2 * LANES), lambda bi, i, hi: (bi, hi, i, 0)),
                   pl.BlockSpec((None, hb, tm, V_DIM), lambda bi, i, hi: (bi, hi, i, 0))],
        out_shape=[jax.ShapeDtypeStruct((b, h, lt, wq), BF16),
                   jax.ShapeDtypeStruct((b, h, lt, 2 * LANES), BF16),
                   jax.ShapeDtypeStruct((b, h, lt, V_DIM), BF16)],
        compiler_params=_params(3),
        name="mla_qkv_up",
    )(cqkv, cqkv, w_q, w_kv, cos_t, sin_t, kr)


def _attn_kernel(q_ref, k_ref, v_ref, g_ref, o_ref, s_ref, p_ref, m_ref, l_ref, acc_ref, *, tk, sub):
    tq = q_ref.shape[0]
    n_k = k_ref.shape[0] // tk
    n_grp = tk // LANES
    n_sub = tq // sub
    m_ref[...] = jnp.full_like(m_ref, -jnp.inf)
    l_ref[...] = jnp.zeros_like(l_ref)
    acc_ref[...] = jnp.zeros_like(acc_ref)

    def key_rows(t):
        return pl.ds(pl.multiple_of(t * tk, tk), tk)

    def scores(t, slot, i):
        r = slice(i * sub, (i + 1) * sub)
        s_ref[slot, r, :] = _dot_nt(q_ref[r, :], k_ref[key_rows(t), :])

    def update(t, slot, i):
        for c in range(sub // ATTN_CHUNK_ROWS):
            r = slice(i * sub + c * ATTN_CHUNK_ROWS, i * sub + (c + 1) * ATTN_CHUNK_ROWS)
            groups = [s_ref[slot, r, j * LANES:(j + 1) * LANES] for j in range(n_grp)]
            m_part = functools.reduce(jnp.maximum, groups)
            m_prev = m_ref[r, :]
            m_new = jnp.maximum(m_prev, jnp.max(m_part, axis=-1, keepdims=True))
            alpha = jnp.exp2(m_prev - m_new)
            l_sum = alpha * l_ref[r, :]
            for j, sg in enumerate(groups):
                pj = jnp.exp2(sg - m_new)
                l_sum = l_sum + pj
                p_ref[r, j * LANES:(j + 1) * LANES] = pj.astype(BF16)
            l_ref[r, :] = l_sum
            m_ref[r, :] = m_new
            acc_ref[r, :] = alpha * acc_ref[r, :]
        r = slice(i * sub, (i + 1) * sub)
        acc_ref[r, :] += _dot(p_ref[r, :], v_ref[key_rows(t), :])

    def step(t, slot):
        for i in range(n_sub):
            scores(t + 1, 1 - slot, i)
            update(t, slot, i)

    for i in range(n_sub):
        scores(0, 0, i)

    for t in range(n_k - 1):
        step(t, t % 2)
    for i in range(n_sub):
        update(n_k - 1, (n_k - 1) % 2, i)
    l = jnp.sum(l_ref[...], axis=-1, keepdims=True)
    o_ref[...] = (acc_ref[...] / l * g_ref[...].astype(F32)).astype(o_ref.dtype)


def _attention_latent(qh, kh, vh, gate, n_latent):
    b, h, lt, wq = qh.shape
    tq, tk = ATTN_Q_TILE, ATTN_K_TILE
    return pl.pallas_call(
        functools.partial(_attn_kernel, tk=tk, sub=min(ATTN_SUB_ROWS, tq)),
        grid=(b, h, n_latent // tq),
        in_specs=[pl.BlockSpec((None, None, tq, wq), lambda bi, hi, i: (bi, hi, i, 0)),
                  pl.BlockSpec((None, None, lt, wq), lambda bi, hi, i: (bi, hi, 0, 0)),
                  pl.BlockSpec((None, None, lt, V_DIM), lambda bi, hi, i: (bi, hi, 0, 0)),
                  pl.BlockSpec((None, tq, V_DIM), lambda bi, hi, i: (bi, i, hi))],
        out_specs=pl.BlockSpec((None, tq, V_DIM), lambda bi, hi, i: (bi, i, hi)),
        out_shape=jax.ShapeDtypeStruct((b, n_latent, h * V_DIM), BF16),
        scratch_shapes=[pltpu.VMEM((2, tq, tk), F32), pltpu.VMEM((tq, tk), BF16),
                        pltpu.VMEM((tq, LANES), F32),
                        pltpu.VMEM((tq, LANES), F32), pltpu.VMEM((tq, V_DIM), F32)],
        compiler_params=_params(3),
        name="mla_attention",
    )(qh, kh, vh, gate)


def _attn_ctx_kernel(q_ref, k_ref, v_ref, g_ref, o_ref):
    s = _dot_nt(q_ref[...], k_ref[...])
    m = jnp.max(s, axis=-1, keepdims=True)
    pexp = jnp.exp2(s - m)
    l = jnp.sum(pexp, axis=-1, keepdims=True)
    o = _dot(pexp.astype(BF16), v_ref[...])
    o_ref[...] = (o / l * g_ref[...].astype(F32)).astype(o_ref.dtype)


def _attention_ctx(qh, kh, vh, gate, n_latent):
    b, h, lt, wq = qh.shape
    lc = lt - n_latent
    blk = n_latent // lc
    return pl.pallas_call(
        _attn_ctx_kernel,
        grid=(b, h),
        in_specs=[pl.BlockSpec((None, None, lc, wq), lambda bi, hi: (bi, hi, blk, 0)),
                  pl.BlockSpec((None, None, lc, wq), lambda bi, hi: (bi, hi, blk, 0)),
                  pl.BlockSpec((None, None, lc, V_DIM), lambda bi, hi: (bi, hi, blk, 0)),
                  pl.BlockSpec((None, lc, V_DIM), lambda bi, hi: (bi, blk, hi))],
        out_specs=pl.BlockSpec((None, lc, V_DIM), lambda bi, hi: (bi, 0, hi)),
        out_shape=jax.ShapeDtypeStruct((b, lc, h * V_DIM), BF16),
        compiler_params=_params(2),
        name="mla_attention_ctx",
    )(qh, kh, vh, gate)


def _rope_tables(n_latent, n_ctx):
    t = jnp.arange(n_latent)
    rr = (t // GRID_W).astype(F32)
    cc = (t % GRID_W).astype(F32)
    n_freq = QK_ROPE // 4
    inv_freq = ROPE_BASE ** (-jnp.arange(n_freq, dtype=F32) / n_freq)
    ang = jnp.concatenate([rr[:, None] * inv_freq, cc[:, None] * inv_freq], -1)
    cos = jnp.repeat(jnp.cos(ang), 2, axis=-1)
    sin = jnp.repeat(jnp.sin(ang), 2, axis=-1)
    cos = jnp.concatenate([cos, jnp.ones((n_ctx, QK_ROPE), F32)], 0)
    sin = jnp.concatenate([sin, jnp.zeros((n_ctx, QK_ROPE), F32)], 0)
    pad = jnp.zeros((n_latent + n_ctx, LANES - QK_ROPE), F32)
    return jnp.concatenate([cos, pad], 1), jnp.concatenate([sin, pad], 1)


def _rotate_cols(w):
    wp = w.reshape(w.shape[:-1] + (w.shape[-1] // 2, 2))
    return jnp.stack([-wp[..., 1], wp[..., 0]], -1).reshape(w.shape)


def _ssd_layer(xs, mod, n_latent, w_in, conv_w, conv_b, dt_bias, a_log, d_skip, norm_w):
    heads = dt_bias.shape[-1]
    d_inner = heads * SSD_HEAD_DIM
    conv_ch = conv_w.shape[-1]
    w_z = w_in[:, :d_inner].astype(BF16)
    w_xbc = w_in[:, d_inner:d_inner + conv_ch].astype(BF16)
    w_dt_t = w_in[:, d_inner + conv_ch:].T.astype(BF16)
    z_act = _inproj(xs, mod, w_z, n_latent, tn=1024, epilogue="silu", name="ssd_in_z")
    xbc_pre = _inproj(xs, mod, w_xbc, n_latent, tn=1024, name="ssd_in_xbc")
    dt_t = _dt_proj(xs, mod, w_dt_t, dt_bias.reshape(2 * heads, 1), n_latent)
    xbc = _conv_silu(xbc_pre, conv_w, conv_b, n_latent)
    a_cols = (-jnp.exp(a_log.astype(F32)) * math.log2(math.e)).reshape(2 * heads, 1)
    y_fwd = _ssd_scan(xbc, dt_t, a_cols, n_latent, reverse=False)
    d_row = jnp.repeat(d_skip, SSD_HEAD_DIM).reshape(1, d_inner)
    return _ssd_scan(xbc, dt_t, a_cols, n_latent, reverse=True,
                     finish_inputs=(y_fwd, z_act, d_row, norm_w.reshape(1, d_inner)))


def _mla_layer(xs, mod, n_latent, w_in, q_norm, kv_norm, w_uq, w_ukv, tables, need_ctx):
    q_rank, kv_rank = q_norm.shape[0], kv_norm.shape[0]
    h = MLA_HEADS
    cos_t, sin_t = tables
    w_lat = w_in[:, :q_rank + kv_rank].astype(BF16)
    w_kr = w_in[:, q_rank + kv_rank:q_rank + kv_rank + QK_ROPE]
    w_kr2 = jnp.concatenate([w_kr, _rotate_cols(w_kr)], 1).astype(BF16)
    w_gate = w_in[:, q_rank + kv_rank + QK_ROPE:].astype(BF16)
    norm_g = jnp.concatenate([q_norm, kv_norm]).reshape(1, -1).astype(F32)
    lat = _inproj(xs, mod, w_lat, n_latent, tn=q_rank, epilogue="rmsnorm", extra=(norm_g,),
                  name="mla_in_latents")
    kr = _inproj(xs, mod, w_kr2, n_latent, tn=LANES, epilogue="rope", extra=(cos_t, sin_t),
                 name="mla_in_kr")
    gate = _inproj(xs, mod, w_gate, n_latent, tn=1024, epilogue="silu", name="mla_in_gate")
    scale = (QK_NOPE + QK_ROPE) ** -0.5 * math.log2(math.e)
    wq = (w_uq * scale).reshape(q_rank, h, QK_NOPE + QK_ROPE).transpose(1, 0, 2)
    wq = jnp.concatenate([wq, _rotate_cols(wq[..., QK_NOPE:])], -1).astype(BF16)
    wkv = w_ukv.reshape(kv_rank, h, QK_NOPE + V_DIM).transpose(1, 0, 2).astype(BF16)
    qh, kh, vh = _qkv_up(lat, wq, wkv, cos_t, sin_t, kr)
    o = _attention_latent(qh, kh, vh, gate, n_latent)
    if need_ctx:
        o = jnp.concatenate([o, _attention_ctx(qh, kh, vh, gate, n_latent)], axis=1)
    return o


def kernel(x, c, ctx, c_ctx, ada_w, ada_b, ln_g, ln_b, ssd_w_in, ssd_conv_w, ssd_conv_b, ssd_dt_bias, ssd_a_log, ssd_d, ssd_norm_w, ssd_w_out, mla_w_in, mla_q_norm, mla_kv_norm, mla_w_uq, mla_w_ukv, mla_w_o):
    depth = ada_w.shape[0]
    b, n_latent, d = x.shape
    n_ctx = ctx.shape[1]
    alpha = (2 * depth) ** 0.25
    assert b + 1 <= 8

    c_rows = jnp.concatenate([c, c_ctx[None, :], jnp.zeros((8 - b - 1, d), F32)], 0)
    mods = _modulation(c_rows, ada_w, ada_b)
    tables = _rope_tables(n_latent, n_ctx)
    xs = jnp.concatenate([x, ctx], axis=1)

    for i in range(depth):
        last = i == depth - 1
        m = mods[i].reshape(8, 3, d)
        ctx_mod = jnp.broadcast_to(m[b][None], (b, 3, d))
        mod = jnp.concatenate([ctx_mod, m[:b], jnp.zeros((b, 2, d), F32)], 1)
        j = i // 2
        if i % 2 == 0:
            y = _ssd_layer(xs, mod, n_latent, ssd_w_in[j], ssd_conv_w[j], ssd_conv_b[j], ssd_dt_bias[j],
                           ssd_a_log[j], ssd_d[j], ssd_norm_w[j])
            w_out = ssd_w_out[j].astype(BF16)
        else:
            y = _mla_layer(xs, mod, n_latent, mla_w_in[j], mla_q_norm[j], mla_kv_norm[j], mla_w_uq[j],
                           mla_w_ukv[j], tables, need_ctx=not last)
            w_out = mla_w_o[j].astype(BF16)
        rows = n_latent if last else n_latent + n_ctx
        xs = _outproj_norm(y, w_out, xs, mod, ln_g[i], ln_b[i], n_latent, alpha, rows=rows)
    return xs
```

```python
import functools
import math

import jax
import jax.numpy as jnp
from jax import lax
from jax.experimental import pallas as pl
from jax.experimental.pallas import tpu as pltpu

F32 = jnp.float32
BF16 = jnp.bfloat16

GRID_W = 64
SSD_HEAD_DIM = 64
SSD_GROUPS = 8
SSD_STATE = 128
SSD_CHUNK = 128
MLA_HEADS = 16
QK_NOPE = 128
QK_ROPE = 64
V_DIM = 128
ROPE_BASE = 10000.0
LN_EPS = 1e-5
RMS_EPS = 1e-6

VMEM_LIMIT_BYTES = 56 * 1024 * 1024
LANES = 128

ROW_TILE = 768
LATENT_TILE = 512
CONV_TILE = 256
CONV_CH_TILE = 2048
SCAN_TILE = 256
SCAN_GROUPS = 2
UP_HEADS_PER_STEP = 4
ATTN_Q_TILE = 1024
ATTN_K_TILE = 768
ATTN_SUB_ROWS = 256
ATTN_CHUNK_ROWS = 32


def _tile(limit, *sizes):
    return max(t for t in range(LANES, limit + 1, LANES) if all(sz % t == 0 for sz in sizes))


def _params(n_axes):
    return pltpu.CompilerParams(dimension_semantics=("arbitrary",) * n_axes,
                                vmem_limit_bytes=VMEM_LIMIT_BYTES)


def _silu(v):
    return v * (1.0 / (1.0 + jnp.exp(-v)))


def _softplus(v):
    return jnp.maximum(v, 0.0) + jnp.log(1.0 + jnp.exp(-jnp.abs(v)))


def _dot(a, b):
    return jnp.dot(a, b, preferred_element_type=F32)


def _dot_nt(a, b):
    return lax.dot_general(a, b, (((1,), (1,)), ((), ())), preferred_element_type=F32)


def _mod_kernel(c_ref, w_ref, b_ref, o_ref):
    a = _silu(c_ref[...])
    o_ref[...] = jnp.dot(a, w_ref[...], preferred_element_type=F32,
                         precision=lax.Precision.HIGHEST) + b_ref[...]


def _modulation(c_rows, ada_w, ada_b):
    depth, d, n = ada_w.shape
    tn = 768 if n % 768 == 0 else n
    return pl.pallas_call(
        _mod_kernel,
        grid=(depth, n // tn),
        in_specs=[pl.BlockSpec((8, d), lambda i, j: (0, 0)),
                  pl.BlockSpec((None, d, tn), lambda i, j: (i, 0, j)),
                  pl.BlockSpec((None, 1, tn), lambda i, j: (i, 0, j))],
        out_specs=pl.BlockSpec((None, 8, tn), lambda i, j: (i, 0, j)),
        out_shape=jax.ShapeDtypeStruct((depth, 8, n), F32),
        compiler_params=_params(2),
        name="adaln_modulation",
    )(c_rows, ada_w, ada_b.reshape(depth, 1, n))


def _modulate(x_ref, mod_ref, u_ref, row0, n_latent):
    tm = x_ref.shape[0]
    rows = row0 + lax.broadcasted_iota(jnp.int32, (tm, 1), 0)
    is_ctx = rows >= n_latent
    shift = jnp.where(is_ctx, mod_ref[0:1, :], mod_ref[3:4, :])
    scale = jnp.where(is_ctx, mod_ref[1:2, :], mod_ref[4:5, :])
    u_ref[...] = (x_ref[...] * (1.0 + scale) + shift).astype(BF16)


def _inproj_kernel(x_ref, mod_ref, w_ref, g_ref, cos_ref, sin_ref, o_ref, u_ref, *, n_latent, kinds):
    i, j = pl.program_id(1), pl.program_id(2)
    tm = x_ref.shape[0]

    @pl.when(j == 0)
    def _():
        _modulate(x_ref, mod_ref, u_ref, i * tm, n_latent)

    def tile(kind):
        acc = _dot(u_ref[...], w_ref[...])
        if kind == "silu":
            acc = _silu(acc)
        elif kind == "rmsnorm":
            ms = jnp.mean(acc * acc, axis=-1, keepdims=True)
            acc = acc * lax.rsqrt(ms + RMS_EPS) * g_ref[...]
        if kind == "rope":
            a = acc[:, :LANES]
            o_ref[:, :LANES] = (a * cos_ref[...] + pltpu.roll(a, 64, 1) * sin_ref[...]).astype(o_ref.dtype)
            if acc.shape[1] > LANES:
                o_ref[:, LANES:] = acc[:, LANES:].astype(o_ref.dtype)
        else:
            o_ref[...] = acc.astype(o_ref.dtype)

    for kind in dict.fromkeys(kinds):
        js = [t for t, kd in enumerate(kinds) if kd == kind]
        assert js == list(range(js[0], js[-1] + 1))
        pl.when(jnp.logical_and(j >= js[0], j <= js[-1]))(functools.partial(tile, kind))


def _inproj(x, mod, w, n_latent, *, tn, kinds, norm_g=None, tables=None, name):
    b, lt, d = x.shape
    n = w.shape[1]
    tm = ROW_TILE
    assert len(kinds) * tn == n
    if norm_g is None:
        norm_g = jnp.ones((1, n), F32)
    if tables is None:
        tables = (jnp.zeros((lt, LANES), F32),) * 2
    return pl.pallas_call(
        functools.partial(_inproj_kernel, n_latent=n_latent, kinds=tuple(kinds)),
        grid=(b, lt // tm, n // tn),
        in_specs=[pl.BlockSpec((None, tm, d), lambda bi, i, j: (bi, i, 0)),
                  pl.BlockSpec((None, 8, d), lambda bi, i, j: (bi, 0, 0)),
                  pl.BlockSpec((d, tn), lambda bi, i, j: (0, j)),
                  pl.BlockSpec((1, tn), lambda bi, i, j: (0, j)),
                  pl.BlockSpec((tm, LANES), lambda bi, i, j: (i, 0)),
                  pl.BlockSpec((tm, LANES), lambda bi, i, j: (i, 0))],
        out_specs=pl.BlockSpec((None, tm, tn), lambda bi, i, j: (bi, i, j)),
        out_shape=jax.ShapeDtypeStruct((b, lt, n), BF16),
        scratch_shapes=[pltpu.VMEM((tm, d), BF16)],
        compiler_params=_params(3),
        name=name,
    )(x, mod, w, norm_g, *tables)


def _dt_kernel(x_ref, mod_ref, w_ref, bias_ref, o_ref, u_ref, *, n_latent):
    i = pl.program_id(1)
    _modulate(x_ref, mod_ref, u_ref, i * x_ref.shape[0], n_latent)
    raw = _dot_nt(w_ref[...], u_ref[...])
    o_ref[...] = _softplus(raw + bias_ref[...])


def _dt_proj(x, mod, w_t, bias, n_latent):
    b, lt, d = x.shape
    nh = w_t.shape[0]
    tm = ROW_TILE
    return pl.pallas_call(
        functools.partial(_dt_kernel, n_latent=n_latent),
        grid=(b, lt // tm),
        in_specs=[pl.BlockSpec((None, tm, d), lambda bi, i: (bi, i, 0)),
                  pl.BlockSpec((None, 8, d), lambda bi, i: (bi, 0, 0)),
                  pl.BlockSpec((nh, d), lambda bi, i: (0, 0)),
                  pl.BlockSpec((nh, 1), lambda bi, i: (0, 0))],
        out_specs=pl.BlockSpec((None, nh, tm), lambda bi, i: (bi, 0, i)),
        out_shape=jax.ShapeDtypeStruct((b, nh, lt), F32),
        scratch_shapes=[pltpu.VMEM((tm, d), BF16)],
        compiler_params=_params(2),
        name="ssd_dt_proj",
    )(x, mod, w_t, bias)


def _conv_kernel(cur_ref, prev_ref, next_ref, w_ref, b_ref, o_ref, buf_ref, *, n_latent, n_rows):
    i = pl.program_id(1)
    tb = cur_ref.shape[0]
    taps = w_ref.shape[0]
    pad = taps // 2
    start = i * tb
    prev_ok = jnp.logical_and(start != 0, start != n_latent)
    next_ok = jnp.logical_and(start + tb != n_latent, start + tb != n_rows)
    buf_ref[0:8, :] = jnp.where(prev_ok, prev_ref[...].astype(F32), 0.0)
    buf_ref[8:8 + tb, :] = cur_ref[...].astype(F32)
    buf_ref[8 + tb:16 + tb, :] = jnp.where(next_ok, next_ref[...].astype(F32), 0.0)
    acc = b_ref[...] + w_ref[0:1, :] * buf_ref[8 - pad:8 - pad + tb, :]
    for k in range(1, taps):
        acc = acc + w_ref[k:k + 1, :] * buf_ref[8 - pad + k:8 - pad + k + tb, :]
    o_ref[...] = _silu(acc).astype(o_ref.dtype)


def _conv_silu(h, conv_w, conv_b, n_latent, col0):
    b, lt, _ = h.shape
    c = conv_w.shape[1]
    tb, cb = CONV_TILE, _tile(CONV_CH_TILE, c, col0)
    j0 = col0 // cb
    r8 = tb // 8
    last8 = lt // 8 - 1
    return pl.pallas_call(
        functools.partial(_conv_kernel, n_latent=n_latent, n_rows=lt),
        grid=(b, lt // tb, c // cb),
        in_specs=[pl.BlockSpec((None, tb, cb), lambda bi, i, j: (bi, i, j0 + j)),
                  pl.BlockSpec((None, 8, cb), lambda bi, i, j: (bi, jnp.maximum(i * r8 - 1, 0), j0 + j)),
                  pl.BlockSpec((None, 8, cb), lambda bi, i, j: (bi, jnp.minimum((i + 1) * r8, last8), j0 + j)),
                  pl.BlockSpec((conv_w.shape[0], cb), lambda bi, i, j: (0, j)),
                  pl.BlockSpec((1, cb), lambda bi, i, j: (0, j))],
        out_specs=pl.BlockSpec((None, tb, cb), lambda bi, i, j: (bi, i, j)),
        out_shape=jax.ShapeDtypeStruct((b, lt, c), BF16),
        scratch_shapes=[pltpu.VMEM((tb + 16, cb), F32)],
        compiler_params=_params(3),
        name="ssd_conv_silu",
    )(h, h, h, conv_w, conv_b.reshape(1, c))


def _scan_kernel(x_ref, b_ref, c_ref, dt_ref, a_ref, *rest, reverse, finish, n_groups):
    if finish:
        yf_ref, z_ref, dskip_ref, nw_ref, o_ref, h_ref = rest
    else:
        o_ref, h_ref = rest
    t = pl.program_id(2)
    q = SSD_CHUNK
    p = SSD_HEAD_DIM
    n = SSD_STATE
    gw = x_ref.shape[1] // n_groups
    n_heads = gw // p
    n_chunks = x_ref.shape[0] // q

    @pl.when(t == 0)
    def _():
        h_ref[...] = jnp.zeros_like(h_ref)

    li = lax.broadcasted_iota(jnp.int32, (q, q), 0)
    ri = lax.broadcasted_iota(jnp.int32, (q, q), 1)
    keep = (ri >= li) if reverse else (ri <= li)
    keep_t = (li >= ri) if reverse else (li <= ri)
    cum_mat = jnp.where(keep_t, 1.0, 0.0).astype(BF16)
    lane = lax.broadcasted_iota(jnp.int32, (q, 2 * p), 1)
    first_half = lane < p
    lo_mask = jnp.where(first_half, 1.0, 0.0).astype(BF16)
    hi_mask = jnp.where(first_half, 0.0, 1.0).astype(BF16)
    sel = (lax.broadcasted_iota(jnp.int32, (n_heads, gw), 1) // p
           == lax.broadcasted_iota(jnp.int32, (n_heads, gw), 0))

    order = range(n_chunks - 1, -1, -1) if reverse else range(n_chunks)
    for ck in order:
        rows = pl.ds(ck * q, q)
        for gi in range(n_groups):
            cols = slice(gi * gw, (gi + 1) * gw)
            hrows = slice(gi * n_heads, (gi + 1) * n_heads)
            xg = x_ref[rows, cols]
            bg = b_ref[rows, gi * n:(gi + 1) * n]
            cg = c_ref[rows, gi * n:(gi + 1) * n]
            dt_t = dt_ref[hrows, rows]
            a_t = dt_t * a_ref[hrows, :]
            a_hi = a_t.astype(BF16)
            r1 = a_t - a_hi.astype(F32)
            a_mid = r1.astype(BF16)
            a_lo = (r1 - a_mid.astype(F32)).astype(BF16)
            cs_t = _dot(a_hi, cum_mat) + _dot(a_mid, cum_mat) + _dot(a_lo, cum_mat)
            tot = jnp.sum(a_t, axis=1, keepdims=True)
            w_t = (dt_t * jnp.exp2(tot - cs_t)).astype(BF16)
            src_t = cs_t - jnp.log2(dt_t)
            decay_row = jnp.sum(jnp.where(sel, jnp.exp2(tot), 0.0), axis=0, keepdims=True)

            cb = _dot_nt(cg, bg).astype(BF16)
            bg_t = bg.astype(F32).T.astype(BF16)
            state = h_ref[gi]
            y_state = _dot(cg, state.astype(BF16))

            y_parts, inc_parts = [], []
            for pr in range(n_heads // 2):
                lhs_top, lhs_bot, cs_cols = [], [], []
                for e in (2 * pr, 2 * pr + 1):
                    cs_col = jnp.sum(jnp.where(keep, a_t[e:e + 1, :], 0.0), axis=1, keepdims=True)
                    seg = jnp.where(keep, cs_col - src_t[e:e + 1, :], -jnp.inf)
                    lhs_top.append(jnp.exp2(seg).astype(BF16) * cb)
                    lhs_bot.append(bg_t * w_t[e:e + 1, :])
                    cs_cols.append(cs_col)
                lhs = jnp.concatenate([jnp.concatenate(lhs_top, axis=1),
                                       jnp.concatenate(lhs_bot, axis=1)], axis=0)
                x_pair = xg[:, 2 * pr * p:(2 * pr + 2) * p]
                rhs = jnp.concatenate([x_pair * lo_mask, x_pair * hi_mask], axis=0)
                out = _dot(lhs, rhs)
                from_start = jnp.exp2(jnp.where(first_half, cs_cols[0], cs_cols[1]))
                y_parts.append(out[:q, :] + y_state[:, 2 * pr * p:(2 * pr + 2) * p] * from_start)
                inc_parts.append(out[q:, :])
            y = jnp.concatenate(y_parts, axis=1)
            h_ref[gi] = state * decay_row + jnp.concatenate(inc_parts, axis=1)

            if finish:
                y = y + yf_ref[rows, cols].astype(F32) + dskip_ref[:, cols] * xg.astype(F32)
                y = y * z_ref[rows, cols].astype(F32)
                ms = jnp.mean(y * y, axis=-1, keepdims=True)
                y = y * lax.rsqrt(ms + RMS_EPS) * nw_ref[:, cols]
            o_ref[rows, cols] = y.astype(o_ref.dtype)


def _ssd_scan(xbc, dt_t, a_cols, n_latent, *, reverse, finish_inputs=None):
    b, lt, _ = xbc.shape
    g, n, p = SSD_GROUPS, SSD_STATE, SSD_HEAD_DIM
    heads = dt_t.shape[1] // 2
    e = heads // g
    ng = SCAN_GROUPS
    gw = ng * e * p
    d_inner = heads * p
    tb = SCAN_TILE
    nblk = lt // tb
    n_lat_blk = n_latent // tb
    direction = 1 if reverse else 0
    gsteps = g // ng
    b_col0 = d_inner // (ng * n)

    def blk(t):
        if reverse:
            return nblk - 1 - t
        return jnp.where(t < nblk - n_lat_blk, n_lat_blk + t, t - (nblk - n_lat_blk))

    in_specs = [pl.BlockSpec((None, tb, gw), lambda bi, gi, t: (bi, blk(t), gi)),
                pl.BlockSpec((None, tb, ng * n), lambda bi, gi, t: (bi, blk(t), b_col0 + gi)),
                pl.BlockSpec((None, tb, ng * n), lambda bi, gi, t: (bi, blk(t), b_col0 + gsteps + gi)),
                pl.BlockSpec((None, ng * e, tb), lambda bi, gi, t: (bi, direction * gsteps + gi, blk(t))),
                pl.BlockSpec((ng * e, 1), lambda bi, gi, t: (direction * gsteps + gi, 0))]
    args = [xbc, xbc, xbc, dt_t, a_cols]
    if finish_inputs is not None:
        y_fwd, z_act, d_skip, norm_w = finish_inputs
        in_specs += [pl.BlockSpec((None, tb, gw), lambda bi, gi, t: (bi, blk(t), gi)),
                     pl.BlockSpec((None, tb, gw), lambda bi, gi, t: (bi, blk(t), gi)),
                     pl.BlockSpec((1, gw), lambda bi, gi, t: (0, gi)),
                     pl.BlockSpec((1, gw), lambda bi, gi, t: (0, gi))]
        args += [y_fwd, z_act, d_skip, norm_w]
    return pl.pallas_call(
        functools.partial(_scan_kernel, reverse=reverse, finish=finish_inputs is not None, n_groups=ng),
        grid=(b, gsteps, nblk),
        in_specs=in_specs,
        out_specs=pl.BlockSpec((None, tb, gw), lambda bi, gi, t: (bi, blk(t), gi)),
        out_shape=jax.ShapeDtypeStruct((b, lt, d_inner), BF16),
        scratch_shapes=[pltpu.VMEM((ng, n, e * p), F32)],
        compiler_params=_params(3),
        name="ssd_scan_bwd" if reverse else "ssd_scan_fwd",
    )(*args)


def _outproj_kernel(y_ref, w_ref, x_ref, mod_ref, g_ref, b_ref, o_ref, acc_ref, *, n_latent, alpha):
    i, k = pl.program_id(1), pl.program_id(2)
    tm = x_ref.shape[0]

    @pl.when(k == 0)
    def _():
        acc_ref[...] = jnp.zeros_like(acc_ref)

    acc_ref[...] += _dot(y_ref[...], w_ref[...])

    @pl.when(k == pl.num_programs(2) - 1)
    def _():
        rows = i * tm + lax.broadcasted_iota(jnp.int32, (tm, 1), 0)
        gate = jnp.where(rows >= n_latent, mod_ref[2:3, :], mod_ref[5:6, :])
        r = alpha * x_ref[...] + gate * acc_ref[...]
        mu = jnp.mean(r, axis=-1, keepdims=True)
        rc = r - mu
        var = jnp.mean(rc * rc, axis=-1, keepdims=True)
        o_ref[...] = rc * lax.rsqrt(var + LN_EPS) * g_ref[...] + b_ref[...]


def _outproj_norm(y, w, x, mod, ln_g, ln_b, n_latent, alpha, *, rows):
    b, _, d = x.shape
    kdim = y.shape[2]
    tm = ROW_TILE if rows % ROW_TILE == 0 else LATENT_TILE
    tk = min(512, kdim)
    return pl.pallas_call(
        functools.partial(_outproj_kernel, n_latent=n_latent, alpha=alpha),
        grid=(b, rows // tm, kdim // tk),
        in_specs=[pl.BlockSpec((None, tm, tk), lambda bi, i, k: (bi, i, k)),
                  pl.BlockSpec((tk, d), lambda bi, i, k: (k, 0)),
                  pl.BlockSpec((None, tm, d), lambda bi, i, k: (bi, i, 0)),
                  pl.BlockSpec((None, 8, d), lambda bi, i, k: (bi, 0, 0)),
                  pl.BlockSpec((1, d), lambda bi, i, k: (0, 0)),
                  pl.BlockSpec((1, d), lambda bi, i, k: (0, 0))],
        out_specs=pl.BlockSpec((None, tm, d), lambda bi, i, k: (bi, i, 0)),
        out_shape=jax.ShapeDtypeStruct((b, rows, d), F32),
        scratch_shapes=[pltpu.VMEM((tm, d), F32)],
        compiler_params=_params(3),
        name="outproj_deepnorm",
    )(y, w, x, mod, ln_g.reshape(1, d), ln_b.reshape(1, d))


def _qkvup_kernel(cq_ref, ckv_ref, wq_ref, wkv_ref, cos_ref, sin_ref, kr_ref, q_ref, k_ref, v_ref):
    cq = cq_ref[...]
    ckv = ckv_ref[...]
    for hh in range(wq_ref.shape[0]):
        acc = _dot(cq, wq_ref[hh])
        hi = acc[:, QK_NOPE:]
        q_ref[hh, :, :QK_NOPE] = acc[:, :QK_NOPE].astype(q_ref.dtype)
        q_ref[hh, :, QK_NOPE:] = (hi * cos_ref[...] + pltpu.roll(hi, 64, 1) * sin_ref[...]).astype(q_ref.dtype)
        acc = _dot(ckv, wkv_ref[hh])
        k_ref[hh, :, :QK_NOPE] = acc[:, :QK_NOPE].astype(k_ref.dtype)
        k_ref[hh, :, QK_NOPE:] = kr_ref[...]
        v_ref[hh, :, :V_DIM] = acc[:, QK_NOPE:].astype(v_ref.dtype)
        v_ref[hh, :, V_DIM:] = jnp.ones((acc.shape[0], V_DIM), v_ref.dtype)


def _qkv_up(cqkv, w_q, w_kv, cos_t, sin_t, kr_col):
    b, lt, _ = cqkv.shape
    h, r, wq = w_q.shape
    tm = ROW_TILE
    hb = min(UP_HEADS_PER_STEP, h)
    return pl.pallas_call(
        _qkvup_kernel,
        grid=(b, lt // tm, h // hb),
        in_specs=[pl.BlockSpec((None, tm, r), lambda bi, i, hi: (bi, i, 0)),
                  pl.BlockSpec((None, tm, r), lambda bi, i, hi: (bi, i, 1)),
                  pl.BlockSpec((hb, r, wq), lambda bi, i, hi: (hi, 0, 0)),
                  pl.BlockSpec((hb, r, wq), lambda bi, i, hi: (hi, 0, 0)),
                  pl.BlockSpec((tm, LANES), lambda bi, i, hi: (i, 0)),
                  pl.BlockSpec((tm, LANES), lambda bi, i, hi: (i, 0)),
                  pl.BlockSpec((None, tm, LANES), lambda bi, i, hi: (bi, i, kr_col))],
        out_specs=[pl.BlockSpec((None, hb, tm, wq), lambda bi, i, hi: (bi, hi, i, 0)),
                   pl.BlockSpec((None, hb, tm, 2 * LANES), lambda bi, i, hi: (bi, hi, i, 0)),
                   pl.BlockSpec((None, hb, tm, 2 * V_DIM), lambda bi, i, hi: (bi, hi, i, 0))],
        out_shape=[jax.ShapeDtypeStruct((b, h, lt, wq), BF16),
                   jax.ShapeDtypeStruct((b, h, lt, 2 * LANES), BF16),
                   jax.ShapeDtypeStruct((b, h, lt, 2 * V_DIM), BF16)],
        compiler_params=_params(3),
        name="mla_qkv_up",
    )(cqkv, cqkv, w_q, w_kv, cos_t, sin_t, cqkv)


def _attn_kernel(q_ref, k_ref, v_ref, g_ref, o_ref, s_ref, p_ref, m_ref, acc_ref, *, tk, sub):
    tq = q_ref.shape[0]
    n_k = k_ref.shape[0] // tk
    n_grp = tk // LANES
    n_sub = tq // sub
    m_ref[...] = jnp.full_like(m_ref, -jnp.inf)
    acc_ref[...] = jnp.zeros_like(acc_ref)

    def key_rows(t):
        return pl.ds(pl.multiple_of(t * tk, tk), tk)

    def scores(t, slot, i):
        r = slice(i * sub, (i + 1) * sub)
        s_ref[slot, r, :] = _dot_nt(q_ref[r, :], k_ref[key_rows(t), :])

    def update(t, slot, i):
        for c in range(sub // ATTN_CHUNK_ROWS):
            r = slice(i * sub + c * ATTN_CHUNK_ROWS, i * sub + (c + 1) * ATTN_CHUNK_ROWS)
            groups = [s_ref[slot, r, j * LANES:(j + 1) * LANES] for j in range(n_grp)]
            m_part = functools.reduce(jnp.maximum, groups)
            m_prev = m_ref[r, :]
            m_new = jnp.maximum(m_prev, jnp.max(m_part, axis=-1, keepdims=True))
            alpha = jnp.exp2(m_prev - m_new)
            for j, sg in enumerate(groups):
                p_ref[r, j * LANES:(j + 1) * LANES] = jnp.exp2(sg - m_new).astype(BF16)
            m_ref[r, :] = m_new
            acc_ref[r, :V_DIM] = alpha * acc_ref[r, :V_DIM]
            acc_ref[r, V_DIM:] = alpha * acc_ref[r, V_DIM:]
        r = slice(i * sub, (i + 1) * sub)
        acc_ref[r, :] += _dot(p_ref[r, :], v_ref[key_rows(t), :])

    def step(t, slot):
        for i in range(n_sub):
            scores(t + 1, 1 - slot, i)
            update(t, slot, i)

    for i in range(n_sub):
        scores(0, 0, i)

    for t in range(n_k - 1):
        step(t, t % 2)
    for i in range(n_sub):
        update(n_k - 1, (n_k - 1) % 2, i)
    o_ref[...] = (acc_ref[:, :V_DIM] / acc_ref[:, V_DIM:] * g_ref[...].astype(F32)).astype(o_ref.dtype)


def _attention_latent(qh, kh, vh, gate, gate_col, n_latent):
    b, h, lt, wq = qh.shape
    tq, tk = ATTN_Q_TILE, ATTN_K_TILE
    return pl.pallas_call(
        functools.partial(_attn_kernel, tk=tk, sub=min(ATTN_SUB_ROWS, tq)),
        grid=(b, h, n_latent // tq),
        in_specs=[pl.BlockSpec((None, None, tq, wq), lambda bi, hi, i: (bi, hi, i, 0)),
                  pl.BlockSpec((None, None, lt, wq), lambda bi, hi, i: (bi, hi, 0, 0)),
                  pl.BlockSpec((None, None, lt, 2 * V_DIM), lambda bi, hi, i: (bi, hi, 0, 0)),
                  pl.BlockSpec((None, tq, V_DIM), lambda bi, hi, i: (bi, i, gate_col + hi))],
        out_specs=pl.BlockSpec((None, tq, V_DIM), lambda bi, hi, i: (bi, i, hi)),
        out_shape=jax.ShapeDtypeStruct((b, n_latent, h * V_DIM), BF16),
        scratch_shapes=[pltpu.VMEM((2, tq, tk), F32), pltpu.VMEM((tq, tk), BF16),
                        pltpu.VMEM((tq, LANES), F32), pltpu.VMEM((tq, 2 * V_DIM), F32)],
        compiler_params=_params(3),
        name="mla_attention",
    )(qh, kh, vh, gate)


def _attn_ctx_kernel(q_ref, k_ref, v_ref, g_ref, o_ref):
    s = _dot_nt(q_ref[...], k_ref[...])
    m = jnp.max(s, axis=-1, keepdims=True)
    pexp = jnp.exp2(s - m)
    o = _dot(pexp.astype(BF16), v_ref[...])
    o_ref[...] = (o[:, :V_DIM] / o[:, V_DIM:] * g_ref[...].astype(F32)).astype(o_ref.dtype)


def _attention_ctx(qh, kh, vh, gate, gate_col, n_latent):
    b, h, lt, wq = qh.shape
    lc = lt - n_latent
    blk = n_latent // lc
    return pl.pallas_call(
        _attn_ctx_kernel,
        grid=(b, h),
        in_specs=[pl.BlockSpec((None, None, lc, wq), lambda bi, hi: (bi, hi, blk, 0)),
                  pl.BlockSpec((None, None, lc, wq), lambda bi, hi: (bi, hi, blk, 0)),
                  pl.BlockSpec((None, None, lc, 2 * V_DIM), lambda bi, hi: (bi, hi, blk, 0)),
                  pl.BlockSpec((None, lc, V_DIM), lambda bi, hi: (bi, blk, gate_col + hi))],
        out_specs=pl.BlockSpec((None, lc, V_DIM), lambda bi, hi: (bi, 0, hi)),
        out_shape=jax.ShapeDtypeStruct((b, lc, h * V_DIM), BF16),
        compiler_params=_params(2),
        name="mla_attention_ctx",
    )(qh, kh, vh, gate)


def _rope_tables(n_latent, n_ctx):
    t = jnp.arange(n_latent)
    rr = (t // GRID_W).astype(F32)
    cc = (t % GRID_W).astype(F32)
    n_freq = QK_ROPE // 4
    inv_freq = ROPE_BASE ** (-jnp.arange(n_freq, dtype=F32) / n_freq)
    ang = jnp.concatenate([rr[:, None] * inv_freq, cc[:, None] * inv_freq], -1)
    cos = jnp.repeat(jnp.cos(ang), 2, axis=-1)
    sin = jnp.repeat(jnp.sin(ang), 2, axis=-1)
    cos = jnp.concatenate([cos, jnp.ones((n_ctx, QK_ROPE), F32)], 0)
    sin = jnp.concatenate([sin, jnp.zeros((n_ctx, QK_ROPE), F32)], 0)
    pad = jnp.zeros((n_latent + n_ctx, LANES - QK_ROPE), F32)
    return jnp.concatenate([cos, pad], 1), jnp.concatenate([sin, pad], 1)


def _rotate_cols(w):
    wp = w.reshape(w.shape[:-1] + (w.shape[-1] // 2, 2))
    return jnp.stack([-wp[..., 1], wp[..., 0]], -1).reshape(w.shape)


def _ssd_layer(xs, mod, n_latent, w_in, conv_w, conv_b, dt_bias, a_log, d_skip, norm_w):
    heads = dt_bias.shape[-1]
    d_inner = heads * SSD_HEAD_DIM
    conv_ch = conv_w.shape[-1]
    w_zx = w_in[:, :d_inner + conv_ch].astype(BF16)
    w_dt_t = w_in[:, d_inner + conv_ch:].T.astype(BF16)
    tn = _tile(1024, d_inner, conv_ch)
    h_zx = _inproj(xs, mod, w_zx, n_latent, tn=tn, kinds=["silu"] * (d_inner // tn) + ["none"] * (conv_ch // tn),
                   name="ssd_in_zx")
    dt_t = _dt_proj(xs, mod, w_dt_t, dt_bias.reshape(2 * heads, 1), n_latent)
    xbc = _conv_silu(h_zx, conv_w, conv_b, n_latent, d_inner)
    a_cols = (-jnp.exp(a_log.astype(F32)) * math.log2(math.e)).reshape(2 * heads, 1)
    y_fwd = _ssd_scan(xbc, dt_t, a_cols, n_latent, reverse=False)
    d_row = jnp.repeat(d_skip, SSD_HEAD_DIM).reshape(1, d_inner)
    return _ssd_scan(xbc, dt_t, a_cols, n_latent, reverse=True,
                     finish_inputs=(y_fwd, h_zx, d_row, norm_w.reshape(1, d_inner)))


def _mla_layer(xs, mod, n_latent, w_in, q_norm, kv_norm, w_uq, w_ukv, tables, need_ctx):
    q_rank, kv_rank = q_norm.shape[0], kv_norm.shape[0]
    h = MLA_HEADS
    cos_t, sin_t = tables
    assert q_rank == kv_rank and q_rank % LANES == 0
    tn = q_rank
    w_kr = w_in[:, 2 * tn:2 * tn + QK_ROPE]
    w_gate = w_in[:, 2 * tn + QK_ROPE:]
    n_gate = w_gate.shape[1]
    w_all = jnp.concatenate([w_in[:, :2 * tn], w_kr, _rotate_cols(w_kr),
                             jnp.zeros((w_in.shape[0], tn - 2 * QK_ROPE), F32), w_gate], 1).astype(BF16)
    norm_g = jnp.concatenate([q_norm, kv_norm, jnp.ones((tn + n_gate,), F32)]).reshape(1, -1)
    h_all = _inproj(xs, mod, w_all, n_latent, tn=tn, kinds=["rmsnorm"] * 2 + ["rope"] + ["silu"] * (n_gate // tn),
                    norm_g=norm_g, tables=tables, name="mla_in_proj")
    kr_col, gate_col = 2 * tn // LANES, 3 * tn // V_DIM
    scale = (QK_NOPE + QK_ROPE) ** -0.5 * math.log2(math.e)
    wq = (w_uq * scale).reshape(q_rank, h, QK_NOPE + QK_ROPE).transpose(1, 0, 2)
    wq = jnp.concatenate([wq, _rotate_cols(wq[..., QK_NOPE:])], -1).astype(BF16)
    wkv = w_ukv.reshape(kv_rank, h, QK_NOPE + V_DIM).transpose(1, 0, 2).astype(BF16)
    qh, kh, vh = _qkv_up(h_all, wq, wkv, cos_t, sin_t, kr_col)
    o = _attention_latent(qh, kh, vh, h_all, gate_col, n_latent)
    if need_ctx:
        o = jnp.concatenate([o, _attention_ctx(qh, kh, vh, h_all, gate_col, n_latent)], axis=1)
    return o


def kernel(x, c, ctx, c_ctx, ada_w, ada_b, ln_g, ln_b, ssd_w_in, ssd_conv_w, ssd_conv_b, ssd_dt_bias, ssd_a_log, ssd_d, ssd_norm_w, ssd_w_out, mla_w_in, mla_q_norm, mla_kv_norm, mla_w_uq, mla_w_ukv, mla_w_o):
    depth = ada_w.shape[0]
    b, n_latent, d = x.shape
    n_ctx = ctx.shape[1]
    alpha = (2 * depth) ** 0.25
    assert b + 1 <= 8

    c_rows = jnp.concatenate([c, c_ctx[None, :], jnp.zeros((8 - b - 1, d), F32)], 0)
    mods = _modulation(c_rows, ada_w, ada_b)
    tables = _rope_tables(n_latent, n_ctx)
    xs = jnp.concatenate([x, ctx], axis=1)

    for i in range(depth):
        last = i == depth - 1
        m = mods[i].reshape(8, 3, d)
        ctx_mod = jnp.broadcast_to(m[b][None], (b, 3, d))
        mod = jnp.concatenate([ctx_mod, m[:b], jnp.zeros((b, 2, d), F32)], 1)
        j = i // 2
        if i % 2 == 0:
            y = _ssd_layer(xs, mod, n_latent, ssd_w_in[j], ssd_conv_w[j], ssd_conv_b[j], ssd_dt_bias[j],
                           ssd_a_log[j], ssd_d[j], ssd_norm_w[j])
            w_out = ssd_w_out[j].astype(BF16)
        else:
            y = _mla_layer(xs, mod, n_latent, mla_w_in[j], mla_q_norm[j], mla_kv_norm[j], mla_w_uq[j],
                           mla_w_ukv[j], tables, need_ctx=not last)
            w_out = mla_w_o[j].astype(BF16)
        rows = n_latent if last else n_latent + n_ctx
        xs = _outproj_norm(y, w_out, xs, mod, ln_g[i], ln_b[i], n_latent, alpha, rows=rows)
    return xs
```

```python
import functools
import math

import jax
import jax.numpy as jnp
from jax import lax
from jax.experimental import pallas as pl
from jax.experimental.pallas import tpu as pltpu

F32 = jnp.float32
BF16 = jnp.bfloat16

GRID_W = 64
SSD_HEAD_DIM = 64
SSD_GROUPS = 8
SSD_STATE = 128
SSD_CHUNK = 128
MLA_HEADS = 16
QK_NOPE = 128
QK_ROPE = 64
V_DIM = 128
ROPE_BASE = 10000.0
LN_EPS = 1e-5
RMS_EPS = 1e-6

VMEM_LIMIT_BYTES = 56 * 1024 * 1024
LANES = 128

ROW_TILE = 768
LATENT_TILE = 512
CONV_TILE = 256
CONV_CH_TILE = 2048
SCAN_TILE = 256
SCAN_GROUPS = 8
UP_HEADS_PER_STEP = 4
ATTN_Q_TILE = 1024
ATTN_K_TILE = 768
ATTN_SUB_ROWS = 512
ATTN_CHUNK_ROWS = 32


def _tile(limit, *sizes):
    return max(t for t in range(LANES, limit + 1, LANES) if all(sz % t == 0 for sz in sizes))


def _params(n_axes):
    return pltpu.CompilerParams(dimension_semantics=("arbitrary",) * n_axes,
                                vmem_limit_bytes=VMEM_LIMIT_BYTES)


def _silu(v):
    return v * (1.0 / (1.0 + jnp.exp(-v)))


def _softplus(v):
    return jnp.maximum(v, 0.0) + jnp.log(1.0 + jnp.exp(-jnp.abs(v)))


def _dot(a, b):
    return jnp.dot(a, b, preferred_element_type=F32)


def _dot_nt(a, b):
    return lax.dot_general(a, b, (((1,), (1,)), ((), ())), preferred_element_type=F32)


def _mod_kernel(c_ref, w_ref, b_ref, o_ref):
    a = _silu(c_ref[...])
    o_ref[...] = jnp.dot(a, w_ref[...], preferred_element_type=F32,
                         precision=lax.Precision.HIGHEST) + b_ref[...]


def _modulation(c_rows, ada_w, ada_b):
    depth, d, n = ada_w.shape
    tn = 768 if n % 768 == 0 else n
    return pl.pallas_call(
        _mod_kernel,
        grid=(depth, n // tn),
        in_specs=[pl.BlockSpec((8, d), lambda i, j: (0, 0)),
                  pl.BlockSpec((None, d, tn), lambda i, j: (i, 0, j)),
                  pl.BlockSpec((None, 1, tn), lambda i, j: (i, 0, j))],
        out_specs=pl.BlockSpec((None, 8, tn), lambda i, j: (i, 0, j)),
        out_shape=jax.ShapeDtypeStruct((depth, 8, n), F32),
        compiler_params=_params(2),
        name="adaln_modulation",
    )(c_rows, ada_w, ada_b.reshape(depth, 1, n))


def _modulate(x_ref, mod_ref, u_ref, row0, n_latent):
    tm = x_ref.shape[0]
    rows = row0 + lax.broadcasted_iota(jnp.int32, (tm, 1), 0)
    is_ctx = rows >= n_latent
    shift = jnp.where(is_ctx, mod_ref[0:1, :], mod_ref[3:4, :])
    scale = jnp.where(is_ctx, mod_ref[1:2, :], mod_ref[4:5, :])
    u_ref[...] = (x_ref[...] * (1.0 + scale) + shift).astype(BF16)


def _inproj_kernel(x_ref, mod_ref, w_ref, g_ref, cos_ref, sin_ref, o_ref, u_ref, *, n_latent, kinds):
    i, j = pl.program_id(1), pl.program_id(2)
    tm = x_ref.shape[0]

    @pl.when(j == 0)
    def _():
        _modulate(x_ref, mod_ref, u_ref, i * tm, n_latent)

    def tile(kind):
        acc = _dot(u_ref[...], w_ref[...])
        if kind == "silu":
            acc = _silu(acc)
        elif kind == "rmsnorm":
            ms = jnp.mean(acc * acc, axis=-1, keepdims=True)
            acc = acc * lax.rsqrt(ms + RMS_EPS) * g_ref[...]
        if kind == "rope":
            a = acc[:, :LANES]
            o_ref[:, :LANES] = (a * cos_ref[...] + pltpu.roll(a, 64, 1) * sin_ref[...]).astype(o_ref.dtype)
            if acc.shape[1] > LANES:
                o_ref[:, LANES:] = acc[:, LANES:].astype(o_ref.dtype)
        else:
            o_ref[...] = acc.astype(o_ref.dtype)

    for kind in dict.fromkeys(kinds):
        js = [t for t, kd in enumerate(kinds) if kd == kind]
        assert js == list(range(js[0], js[-1] + 1))
        pl.when(jnp.logical_and(j >= js[0], j <= js[-1]))(functools.partial(tile, kind))


def _inproj(x, mod, w, n_latent, *, tn, kinds, norm_g=None, tables=None, name):
    b, lt, d = x.shape
    n = w.shape[1]
    tm = ROW_TILE
    assert len(kinds) * tn == n
    if norm_g is None:
        norm_g = jnp.ones((1, n), F32)
    if tables is None:
        tables = (jnp.zeros((lt, LANES), F32),) * 2
    return pl.pallas_call(
        functools.partial(_inproj_kernel, n_latent=n_latent, kinds=tuple(kinds)),
        grid=(b, lt // tm, n // tn),
        in_specs=[pl.BlockSpec((None, tm, d), lambda bi, i, j: (bi, i, 0)),
                  pl.BlockSpec((None, 8, d), lambda bi, i, j: (bi, 0, 0)),
                  pl.BlockSpec((d, tn), lambda bi, i, j: (0, j)),
                  pl.BlockSpec((1, tn), lambda bi, i, j: (0, j)),
                  pl.BlockSpec((tm, LANES), lambda bi, i, j: (i, 0)),
                  pl.BlockSpec((tm, LANES), lambda bi, i, j: (i, 0))],
        out_specs=pl.BlockSpec((None, tm, tn), lambda bi, i, j: (bi, i, j)),
        out_shape=jax.ShapeDtypeStruct((b, lt, n), BF16),
        scratch_shapes=[pltpu.VMEM((tm, d), BF16)],
        compiler_params=_params(3),
        name=name,
    )(x, mod, w, norm_g, *tables)


def _dt_kernel(x_ref, mod_ref, w_ref, bias_ref, o_ref, u_ref, *, n_latent):
    i = pl.program_id(1)
    _modulate(x_ref, mod_ref, u_ref, i * x_ref.shape[0], n_latent)
    raw = _dot_nt(w_ref[...], u_ref[...])
    o_ref[...] = _softplus(raw + bias_ref[...])


def _dt_proj(x, mod, w_t, bias, n_latent):
    b, lt, d = x.shape
    nh = w_t.shape[0]
    tm = ROW_TILE
    return pl.pallas_call(
        functools.partial(_dt_kernel, n_latent=n_latent),
        grid=(b, lt // tm),
        in_specs=[pl.BlockSpec((None, tm, d), lambda bi, i: (bi, i, 0)),
                  pl.BlockSpec((None, 8, d), lambda bi, i: (bi, 0, 0)),
                  pl.BlockSpec((nh, d), lambda bi, i: (0, 0)),
                  pl.BlockSpec((nh, 1), lambda bi, i: (0, 0))],
        out_specs=pl.BlockSpec((None, nh, tm), lambda bi, i: (bi, 0, i)),
        out_shape=jax.ShapeDtypeStruct((b, nh, lt), F32),
        scratch_shapes=[pltpu.VMEM((tm, d), BF16)],
        compiler_params=_params(2),
        name="ssd_dt_proj",
    )(x, mod, w_t, bias)


def _conv_kernel(cur_ref, prev_ref, next_ref, w_ref, b_ref, o_ref, buf_ref, *, n_latent, n_rows):
    i = pl.program_id(1)
    tb = cur_ref.shape[0]
    taps = w_ref.shape[0]
    pad = taps // 2
    start = i * tb
    prev_ok = jnp.logical_and(start != 0, start != n_latent)
    next_ok = jnp.logical_and(start + tb != n_latent, start + tb != n_rows)
    buf_ref[0:8, :] = jnp.where(prev_ok, prev_ref[...].astype(F32), 0.0)
    buf_ref[8:8 + tb, :] = cur_ref[...].astype(F32)
    buf_ref[8 + tb:16 + tb, :] = jnp.where(next_ok, next_ref[...].astype(F32), 0.0)
    acc = b_ref[...] + w_ref[0:1, :] * buf_ref[8 - pad:8 - pad + tb, :]
    for k in range(1, taps):
        acc = acc + w_ref[k:k + 1, :] * buf_ref[8 - pad + k:8 - pad + k + tb, :]
    o_ref[...] = _silu(acc).astype(o_ref.dtype)


def _conv_silu(h, conv_w, conv_b, n_latent, col0):
    b, lt, _ = h.shape
    c = conv_w.shape[1]
    tb, cb = CONV_TILE, _tile(CONV_CH_TILE, c, col0)
    j0 = col0 // cb
    r8 = tb // 8
    last8 = lt // 8 - 1
    return pl.pallas_call(
        functools.partial(_conv_kernel, n_latent=n_latent, n_rows=lt),
        grid=(b, lt // tb, c // cb),
        in_specs=[pl.BlockSpec((None, tb, cb), lambda bi, i, j: (bi, i, j0 + j)),
                  pl.BlockSpec((None, 8, cb), lambda bi, i, j: (bi, jnp.maximum(i * r8 - 1, 0), j0 + j)),
                  pl.BlockSpec((None, 8, cb), lambda bi, i, j: (bi, jnp.minimum((i + 1) * r8, last8), j0 + j)),
                  pl.BlockSpec((conv_w.shape[0], cb), lambda bi, i, j: (0, j)),
                  pl.BlockSpec((1, cb), lambda bi, i, j: (0, j))],
        out_specs=pl.BlockSpec((None, tb, cb), lambda bi, i, j: (bi, i, j)),
        out_shape=jax.ShapeDtypeStruct((b, lt, c), BF16),
        scratch_shapes=[pltpu.VMEM((tb + 16, cb), F32)],
        compiler_params=_params(3),
        name="ssd_conv_silu",
    )(h, h, h, conv_w, conv_b.reshape(1, c))


def _scan_kernel(x_ref, b_ref, c_ref, dt_ref, a_ref, *rest, reverse, finish, n_groups):
    if finish:
        yf_ref, z_ref, dskip_ref, nw_ref, o_ref, h_ref = rest
    else:
        o_ref, h_ref = rest
    t = pl.program_id(2)
    q = SSD_CHUNK
    p = SSD_HEAD_DIM
    n = SSD_STATE
    gw = x_ref.shape[1] // n_groups
    n_heads = gw // p
    n_chunks = x_ref.shape[0] // q

    @pl.when(t == 0)
    def _():
        h_ref[...] = jnp.zeros_like(h_ref)

    li = lax.broadcasted_iota(jnp.int32, (q, q), 0)
    ri = lax.broadcasted_iota(jnp.int32, (q, q), 1)
    keep = (ri >= li) if reverse else (ri <= li)
    keep_t = (li >= ri) if reverse else (li <= ri)
    cum_mat = jnp.where(keep_t, 1.0, 0.0).astype(BF16)
    lane = lax.broadcasted_iota(jnp.int32, (q, 2 * p), 1)
    first_half = lane < p
    lo_mask = jnp.where(first_half, 1.0, 0.0).astype(BF16)
    hi_mask = jnp.where(first_half, 0.0, 1.0).astype(BF16)
    sel = (lax.broadcasted_iota(jnp.int32, (n_heads, gw), 1) // p
           == lax.broadcasted_iota(jnp.int32, (n_heads, gw), 0))

    order = range(n_chunks - 1, -1, -1) if reverse else range(n_chunks)
    for ck in order:
        rows = pl.ds(ck * q, q)
        for gi in range(n_groups):
            cols = slice(gi * gw, (gi + 1) * gw)
            hrows = slice(gi * n_heads, (gi + 1) * n_heads)
            xg = x_ref[rows, cols]
            bg = b_ref[rows, gi * n:(gi + 1) * n]
            cg = c_ref[rows, gi * n:(gi + 1) * n]
            dt_t = dt_ref[hrows, rows]
            a_t = dt_t * a_ref[hrows, :]
            a_hi = a_t.astype(BF16)
            r1 = a_t - a_hi.astype(F32)
            a_mid = r1.astype(BF16)
            a_lo = (r1 - a_mid.astype(F32)).astype(BF16)
            cs_t = _dot(a_hi, cum_mat) + _dot(a_mid, cum_mat) + _dot(a_lo, cum_mat)
            tot = jnp.sum(a_t, axis=1, keepdims=True)
            w_t = (dt_t * jnp.exp2(tot - cs_t)).astype(BF16)
            src_t = cs_t - jnp.log2(dt_t)
            decay_row = jnp.sum(jnp.where(sel, jnp.exp2(tot), 0.0), axis=0, keepdims=True)

            cb = _dot_nt(cg, bg).astype(BF16)
            bg_t = bg.astype(F32).T.astype(BF16)
            state = h_ref[gi]
            y_state = _dot(cg, state.astype(BF16))

            y_parts, inc_parts = [], []
            for pr in range(n_heads // 2):
                lhs_top, lhs_bot, cs_cols = [], [], []
                for e in (2 * pr, 2 * pr + 1):
                    cs_col = jnp.sum(jnp.where(keep, a_t[e:e + 1, :], 0.0), axis=1, keepdims=True)
                    seg = jnp.where(keep, cs_col - src_t[e:e + 1, :], -jnp.inf)
                    lhs_top.append(jnp.exp2(seg).astype(BF16) * cb)
                    lhs_bot.append(bg_t * w_t[e:e + 1, :])
                    cs_cols.append(cs_col)
                lhs = jnp.concatenate([jnp.concatenate(lhs_top, axis=1),
                                       jnp.concatenate(lhs_bot, axis=1)], axis=0)
                x_pair = xg[:, 2 * pr * p:(2 * pr + 2) * p]
                rhs = jnp.concatenate([x_pair * lo_mask, x_pair * hi_mask], axis=0)
                out = _dot(lhs, rhs)
                from_start = jnp.exp2(jnp.where(first_half, cs_cols[0], cs_cols[1]))
                y_parts.append(out[:q, :] + y_state[:, 2 * pr * p:(2 * pr + 2) * p] * from_start)
                inc_parts.append(out[q:, :])
            y = jnp.concatenate(y_parts, axis=1)
            h_ref[gi] = state * decay_row + jnp.concatenate(inc_parts, axis=1)

            if finish:
                y = y + yf_ref[rows, cols].astype(F32) + dskip_ref[:, cols] * xg.astype(F32)
                y = y * z_ref[rows, cols].astype(F32)
                ms = jnp.mean(y * y, axis=-1, keepdims=True)
                y = y * lax.rsqrt(ms + RMS_EPS) * nw_ref[:, cols]
            o_ref[rows, cols] = y.astype(o_ref.dtype)


def _ssd_scan(xbc, dt_t, a_cols, n_latent, *, reverse, finish_inputs=None):
    b, lt, _ = xbc.shape
    g, n, p = SSD_GROUPS, SSD_STATE, SSD_HEAD_DIM
    heads = dt_t.shape[1] // 2
    e = heads // g
    ng = SCAN_GROUPS
    gw = ng * e * p
    d_inner = heads * p
    tb = SCAN_TILE
    nblk = lt // tb
    n_lat_blk = n_latent // tb
    direction = 1 if reverse else 0
    gsteps = g // ng
    b_col0 = d_inner // (ng * n)

    def blk(t):
        if reverse:
            return nblk - 1 - t
        return jnp.where(t < nblk - n_lat_blk, n_lat_blk + t, t - (nblk - n_lat_blk))

    in_specs = [pl.BlockSpec((None, tb, gw), lambda bi, gi, t: (bi, blk(t), gi)),
                pl.BlockSpec((None, tb, ng * n), lambda bi, gi, t: (bi, blk(t), b_col0 + gi)),
                pl.BlockSpec((None, tb, ng * n), lambda bi, gi, t: (bi, blk(t), b_col0 + gsteps + gi)),
                pl.BlockSpec((None, ng * e, tb), lambda bi, gi, t: (bi, direction * gsteps + gi, blk(t))),
                pl.BlockSpec((ng * e, 1), lambda bi, gi, t: (direction * gsteps + gi, 0))]
    args = [xbc, xbc, xbc, dt_t, a_cols]
    if finish_inputs is not None:
        y_fwd, z_act, d_skip, norm_w = finish_inputs
        in_specs += [pl.BlockSpec((None, tb, gw), lambda bi, gi, t: (bi, blk(t), gi)),
                     pl.BlockSpec((None, tb, gw), lambda bi, gi, t: (bi, blk(t), gi)),
                     pl.BlockSpec((1, gw), lambda bi, gi, t: (0, gi)),
                     pl.BlockSpec((1, gw), lambda bi, gi, t: (0, gi))]
        args += [y_fwd, z_act, d_skip, norm_w]
    return pl.pallas_call(
        functools.partial(_scan_kernel, reverse=reverse, finish=finish_inputs is not None, n_groups=ng),
        grid=(b, gsteps, nblk),
        in_specs=in_specs,
        out_specs=pl.BlockSpec((None, tb, gw), lambda bi, gi, t: (bi, blk(t), gi)),
        out_shape=jax.ShapeDtypeStruct((b, lt, d_inner), BF16),
        scratch_shapes=[pltpu.VMEM((ng, n, e * p), F32)],
        compiler_params=_params(3),
        name="ssd_scan_bwd" if reverse else "ssd_scan_fwd",
    )(*args)


def _outproj_kernel(y_ref, w_ref, x_ref, mod_ref, g_ref, b_ref, o_ref, acc_ref, *, n_latent, alpha):
    i, k = pl.program_id(1), pl.program_id(2)
    tm = x_ref.shape[0]

    @pl.when(k == 0)
    def _():
        acc_ref[...] = jnp.zeros_like(acc_ref)

    acc_ref[...] += _dot(y_ref[...], w_ref[...])

    @pl.when(k == pl.num_programs(2) - 1)
    def _():
        rows = i * tm + lax.broadcasted_iota(jnp.int32, (tm, 1), 0)
        gate = jnp.where(rows >= n_latent, mod_ref[2:3, :], mod_ref[5:6, :])
        r = alpha * x_ref[...] + gate * acc_ref[...]
        mu = jnp.mean(r, axis=-1, keepdims=True)
        rc = r - mu
        var = jnp.mean(rc * rc, axis=-1, keepdims=True)
        o_ref[...] = rc * lax.rsqrt(var + LN_EPS) * g_ref[...] + b_ref[...]


def _outproj_norm(y, w, x, mod, ln_g, ln_b, n_latent, alpha, *, rows):
    b, _, d = x.shape
    kdim = y.shape[2]
    tm = ROW_TILE if rows % ROW_TILE == 0 else LATENT_TILE
    tk = min(512, kdim)
    return pl.pallas_call(
        functools.partial(_outproj_kernel, n_latent=n_latent, alpha=alpha),
        grid=(b, rows // tm, kdim // tk),
        in_specs=[pl.BlockSpec((None, tm, tk), lambda bi, i, k: (bi, i, k)),
                  pl.BlockSpec((tk, d), lambda bi, i, k: (k, 0)),
                  pl.BlockSpec((None, tm, d), lambda bi, i, k: (bi, i, 0)),
                  pl.BlockSpec((None, 8, d), lambda bi, i, k: (bi, 0, 0)),
                  pl.BlockSpec((1, d), lambda bi, i, k: (0, 0)),
                  pl.BlockSpec((1, d), lambda bi, i, k: (0, 0))],
        out_specs=pl.BlockSpec((None, tm, d), lambda bi, i, k: (bi, i, 0)),
        out_shape=jax.ShapeDtypeStruct((b, rows, d), F32),
        scratch_shapes=[pltpu.VMEM((tm, d), F32)],
        compiler_params=_params(3),
        name="outproj_deepnorm",
    )(y, w, x, mod, ln_g.reshape(1, d), ln_b.reshape(1, d))


def _qkvup_kernel(cq_ref, ckv_ref, wq_ref, wkv_ref, cos_ref, sin_ref, kr_ref, q_ref, k_ref, v_ref):
    cq = cq_ref[...]
    ckv = ckv_ref[...]
    for hh in range(wq_ref.shape[0]):
        acc = _dot(cq, wq_ref[hh])
        hi = acc[:, QK_NOPE:]
        q_ref[hh, :, :QK_NOPE] = acc[:, :QK_NOPE].astype(q_ref.dtype)
        q_ref[hh, :, QK_NOPE:] = (hi * cos_ref[...] + pltpu.roll(hi, 64, 1) * sin_ref[...]).astype(q_ref.dtype)
        acc = _dot(ckv, wkv_ref[hh])
        k_ref[hh, :, :QK_NOPE] = acc[:, :QK_NOPE].astype(k_ref.dtype)
        k_ref[hh, :, QK_NOPE:] = kr_ref[...]
        v_ref[hh, :, :V_DIM] = acc[:, QK_NOPE:].astype(v_ref.dtype)
        v_ref[hh, :, V_DIM:] = jnp.ones((acc.shape[0], V_DIM), v_ref.dtype)


def _qkv_up(cqkv, w_q, w_kv, cos_t, sin_t, kr_col):
    b, lt, _ = cqkv.shape
    h, r, wq = w_q.shape
    tm = ROW_TILE
    hb = min(UP_HEADS_PER_STEP, h)
    return pl.pallas_call(
        _qkvup_kernel,
        grid=(b, lt // tm, h // hb),
        in_specs=[pl.BlockSpec((None, tm, r), lambda bi, i, hi: (bi, i, 0)),
                  pl.BlockSpec((None, tm, r), lambda bi, i, hi: (bi, i, 1)),
                  pl.BlockSpec((hb, r, wq), lambda bi, i, hi: (hi, 0, 0)),
                  pl.BlockSpec((hb, r, wq), lambda bi, i, hi: (hi, 0, 0)),
                  pl.BlockSpec((tm, LANES), lambda bi, i, hi: (i, 0)),
                  pl.BlockSpec((tm, LANES), lambda bi, i, hi: (i, 0)),
                  pl.BlockSpec((None, tm, LANES), lambda bi, i, hi: (bi, i, kr_col))],
        out_specs=[pl.BlockSpec((None, hb, tm, wq), lambda bi, i, hi: (bi, hi, i, 0)),
                   pl.BlockSpec((None, hb, tm, 2 * LANES), lambda bi, i, hi: (bi, hi, i, 0)),
                   pl.BlockSpec((None, hb, tm, 2 * V_DIM), lambda bi, i, hi: (bi, hi, i, 0))],
        out_shape=[jax.ShapeDtypeStruct((b, h, lt, wq), BF16),
                   jax.ShapeDtypeStruct((b, h, lt, 2 * LANES), BF16),
                   jax.ShapeDtypeStruct((b, h, lt, 2 * V_DIM), BF16)],
        compiler_params=_params(3),
        name="mla_qkv_up",
    )(cqkv, cqkv, w_q, w_kv, cos_t, sin_t, cqkv)


def _attn_kernel(q_ref, k_ref, v_ref, g_ref, o_ref, s_ref, p_ref, m_ref, acc_ref, *, tk, sub):
    tq = q_ref.shape[0]
    n_k = k_ref.shape[0] // tk
    n_grp = tk // LANES
    n_sub = tq // sub
    m_ref[...] = jnp.full_like(m_ref, -jnp.inf)
    acc_ref[...] = jnp.zeros_like(acc_ref)

    def key_rows(t):
        return pl.ds(pl.multiple_of(t * tk, tk), tk)

    def scores(t, slot, i):
        r = slice(i * sub, (i + 1) * sub)
        s_ref[slot, r, :] = _dot_nt(q_ref[r, :], k_ref[key_rows(t), :])

    def update(t, slot, i):
        for c in range(sub // ATTN_CHUNK_ROWS):
            r = slice(i * sub + c * ATTN_CHUNK_ROWS, i * sub + (c + 1) * ATTN_CHUNK_ROWS)
            groups = [s_ref[slot, r, j * LANES:(j + 1) * LANES] for j in range(n_grp)]
            m_part = functools.reduce(jnp.maximum, groups)
            m_prev = m_ref[r, :]
            m_new = jnp.maximum(m_prev, jnp.max(m_part, axis=-1, keepdims=True))
            alpha = jnp.exp2(m_prev - m_new)
            for j, sg in enumerate(groups):
                p_ref[r, j * LANES:(j + 1) * LANES] = jnp.exp2(sg - m_new).astype(BF16)
            m_ref[r, :] = m_new
            acc_ref[r, :V_DIM] = alpha * acc_ref[r, :V_DIM]
            acc_ref[r, V_DIM:] = alpha * acc_ref[r, V_DIM:]
        r = slice(i * sub, (i + 1) * sub)
        acc_ref[r, :] += _dot(p_ref[r, :], v_ref[key_rows(t), :])

    def step(t, slot):
        for i in range(n_sub):
            scores(t + 1, 1 - slot, i)
            update(t, slot, i)

    for i in range(n_sub):
        scores(0, 0, i)

    for t in range(n_k - 1):
        step(t, t % 2)
    for i in range(n_sub):
        update(n_k - 1, (n_k - 1) % 2, i)
    o_ref[...] = (acc_ref[:, :V_DIM] / acc_ref[:, V_DIM:] * g_ref[...].astype(F32)).astype(o_ref.dtype)


def _attention_latent(qh, kh, vh, gate, gate_col, n_latent):
    b, h, lt, wq = qh.shape
    tq, tk = ATTN_Q_TILE, ATTN_K_TILE
    return pl.pallas_call(
        functools.partial(_attn_kernel, tk=tk, sub=min(ATTN_SUB_ROWS, tq)),
        grid=(b, h, n_latent // tq),
        in_specs=[pl.BlockSpec((None, None, tq, wq), lambda bi, hi, i: (bi, hi, i, 0)),
                  pl.BlockSpec((None, None, lt, wq), lambda bi, hi, i: (bi, hi, 0, 0)),
                  pl.BlockSpec((None, None, lt, 2 * V_DIM), lambda bi, hi, i: (bi, hi, 0, 0)),
                  pl.BlockSpec((None, tq, V_DIM), lambda bi, hi, i: (bi, i, gate_col + hi))],
        out_specs=pl.BlockSpec((None, tq, V_DIM), lambda bi, hi, i: (bi, i, hi)),
        out_shape=jax.ShapeDtypeStruct((b, n_latent, h * V_DIM), BF16),
        scratch_shapes=[pltpu.VMEM((2, tq, tk), F32), pltpu.VMEM((tq, tk), BF16),
                        pltpu.VMEM((tq, LANES), F32), pltpu.VMEM((tq, 2 * V_DIM), F32)],
        compiler_params=_params(3),
        name="mla_attention",
    )(qh, kh, vh, gate)


def _attn_ctx_kernel(q_ref, k_ref, v_ref, g_ref, o_ref):
    s = _dot_nt(q_ref[...], k_ref[...])
    m = jnp.max(s, axis=-1, keepdims=True)
    pexp = jnp.exp2(s - m)
    o = _dot(pexp.astype(BF16), v_ref[...])
    o_ref[...] = (o[:, :V_DIM] / o[:, V_DIM:] * g_ref[...].astype(F32)).astype(o_ref.dtype)


def _attention_ctx(qh, kh, vh, gate, gate_col, n_latent):
    b, h, lt, wq = qh.shape
    lc = lt - n_latent
    blk = n_latent // lc
    return pl.pallas_call(
        _attn_ctx_kernel,
        grid=(b, h),
        in_specs=[pl.BlockSpec((None, None, lc, wq), lambda bi, hi: (bi, hi, blk, 0)),
                  pl.BlockSpec((None, None, lc, wq), lambda bi, hi: (bi, hi, blk, 0)),
                  pl.BlockSpec((None, None, lc, 2 * V_DIM), lambda bi, hi: (bi, hi, blk, 0)),
                  pl.BlockSpec((None, lc, V_DIM), lambda bi, hi: (bi, blk, gate_col + hi))],
        out_specs=pl.BlockSpec((None, lc, V_DIM), lambda bi, hi: (bi, 0, hi)),
        out_shape=jax.ShapeDtypeStruct((b, lc, h * V_DIM), BF16),
        compiler_params=_params(2),
        name="mla_attention_ctx",
    )(qh, kh, vh, gate)


def _rope_tables(n_latent, n_ctx):
    t = jnp.arange(n_latent)
    rr = (t // GRID_W).astype(F32)
    cc = (t % GRID_W).astype(F32)
    n_freq = QK_ROPE // 4
    inv_freq = ROPE_BASE ** (-jnp.arange(n_freq, dtype=F32) / n_freq)
    ang = jnp.concatenate([rr[:, None] * inv_freq, cc[:, None] * inv_freq], -1)
    cos = jnp.repeat(jnp.cos(ang), 2, axis=-1)
    sin = jnp.repeat(jnp.sin(ang), 2, axis=-1)
    cos = jnp.concatenate([cos, jnp.ones((n_ctx, QK_ROPE), F32)], 0)
    sin = jnp.concatenate([sin, jnp.zeros((n_ctx, QK_ROPE), F32)], 0)
    pad = jnp.zeros((n_latent + n_ctx, LANES - QK_ROPE), F32)
    return jnp.concatenate([cos, pad], 1), jnp.concatenate([sin, pad], 1)


def _rotate_cols(w):
    wp = w.reshape(w.shape[:-1] + (w.shape[-1] // 2, 2))
    return jnp.stack([-wp[..., 1], wp[..., 0]], -1).reshape(w.shape)


def _ssd_layer(xs, mod, n_latent, w_in, conv_w, conv_b, dt_bias, a_log, d_skip, norm_w):
    heads = dt_bias.shape[-1]
    d_inner = heads * SSD_HEAD_DIM
    conv_ch = conv_w.shape[-1]
    w_zx = w_in[:, :d_inner + conv_ch].astype(BF16)
    w_dt_t = w_in[:, d_inner + conv_ch:].T.astype(BF16)
    tn = _tile(1024, d_inner, conv_ch)
    h_zx = _inproj(xs, mod, w_zx, n_latent, tn=tn, kinds=["silu"] * (d_inner // tn) + ["none"] * (conv_ch // tn),
                   name="ssd_in_zx")
    dt_t = _dt_proj(xs, mod, w_dt_t, dt_bias.reshape(2 * heads, 1), n_latent)
    xbc = _conv_silu(h_zx, conv_w, conv_b, n_latent, d_inner)
    a_cols = (-jnp.exp(a_log.astype(F32)) * math.log2(math.e)).reshape(2 * heads, 1)
    y_fwd = _ssd_scan(xbc, dt_t, a_cols, n_latent, reverse=False)
    d_row = jnp.repeat(d_skip, SSD_HEAD_DIM).reshape(1, d_inner)
    return _ssd_scan(xbc, dt_t, a_cols, n_latent, reverse=True,
                     finish_inputs=(y_fwd, h_zx, d_row, norm_w.reshape(1, d_inner)))


def _mla_layer(xs, mod, n_latent, w_in, q_norm, kv_norm, w_uq, w_ukv, tables, need_ctx):
    q_rank, kv_rank = q_norm.shape[0], kv_norm.shape[0]
    h = MLA_HEADS
    cos_t, sin_t = tables
    assert q_rank == kv_rank and q_rank % LANES == 0
    tn = q_rank
    w_kr = w_in[:, 2 * tn:2 * tn + QK_ROPE]
    w_gate = w_in[:, 2 * tn + QK_ROPE:]
    n_gate = w_gate.shape[1]
    w_all = jnp.concatenate([w_in[:, :2 * tn], w_kr, _rotate_cols(w_kr),
                             jnp.zeros((w_in.shape[0], tn - 2 * QK_ROPE), F32), w_gate], 1).astype(BF16)
    norm_g = jnp.concatenate([q_norm, kv_norm, jnp.ones((tn + n_gate,), F32)]).reshape(1, -1)
    h_all = _inproj(xs, mod, w_all, n_latent, tn=tn, kinds=["rmsnorm"] * 2 + ["rope"] + ["silu"] * (n_gate // tn),
                    norm_g=norm_g, tables=tables, name="mla_in_proj")
    kr_col, gate_col = 2 * tn // LANES, 3 * tn // V_DIM
    scale = (QK_NOPE + QK_ROPE) ** -0.5 * math.log2(math.e)
    wq = (w_uq * scale).reshape(q_rank, h, QK_NOPE + QK_ROPE).transpose(1, 0, 2)
    wq = jnp.concatenate([wq, _rotate_cols(wq[..., QK_NOPE:])], -1).astype(BF16)
    wkv = w_ukv.reshape(kv_rank, h, QK_NOPE + V_DIM).transpose(1, 0, 2).astype(BF16)
    qh, kh, vh = _qkv_up(h_all, wq, wkv, cos_t, sin_t, kr_col)
    o = _attention_latent(qh, kh, vh, h_all, gate_col, n_latent)
    if need_ctx:
        o = jnp.concatenate([o, _attention_ctx(qh, kh, vh, h_all, gate_col, n_latent)], axis=1)
    return o


def kernel(x, c, ctx, c_ctx, ada_w, ada_b, ln_g, ln_b, ssd_w_in, ssd_conv_w, ssd_conv_b, ssd_dt_bias, ssd_a_log, ssd_d, ssd_norm_w, ssd_w_out, mla_w_in, mla_q_norm, mla_kv_norm, mla_w_uq, mla_w_ukv, mla_w_o):
    depth = ada_w.shape[0]
    b, n_latent, d = x.shape
    n_ctx = ctx.shape[1]
    alpha = (2 * depth) ** 0.25
    assert b + 1 <= 8

    c_rows = jnp.concatenate([c, c_ctx[None, :], jnp.zeros((8 - b - 1, d), F32)], 0)
    mods = _modulation(c_rows, ada_w, ada_b)
    tables = _rope_tables(n_latent, n_ctx)
    xs = jnp.concatenate([x, ctx], axis=1)

    for i in range(depth):
        last = i == depth - 1
        m = mods[i].reshape(8, 3, d)
        ctx_mod = jnp.broadcast_to(m[b][None], (b, 3, d))
        mod = jnp.concatenate([ctx_mod, m[:b], jnp.zeros((b, 2, d), F32)], 1)
        j = i // 2
        if i % 2 == 0:
            y = _ssd_layer(xs, mod, n_latent, ssd_w_in[j], ssd_conv_w[j], ssd_conv_b[j], ssd_dt_bias[j],
                           ssd_a_log[j], ssd_d[j], ssd_norm_w[j])
            w_out = ssd_w_out[j].astype(BF16)
        else:
            y = _mla_layer(xs, mod, n_latent, mla_w_in[j], mla_q_norm[j], mla_kv_norm[j], mla_w_uq[j],
                           mla_w_ukv[j], tables, need_ctx=not last)
            w_out = mla_w_o[j].astype(BF16)
        rows = n_latent if last else n_latent + n_ctx
        xs = _outproj_norm(y, w_out, xs, mod, ln_g[i], ln_b[i], n_latent, alpha, rows=rows)
    return xs
```

```python
import functools
import math

import jax
import jax.numpy as jnp
from jax import lax
from jax.experimental import pallas as pl
from jax.experimental.pallas import tpu as pltpu

F32 = jnp.float32
BF16 = jnp.bfloat16

GRID_W = 64
SSD_HEAD_DIM = 64
SSD_GROUPS = 8
SSD_STATE = 128
SSD_CHUNK = 128
MLA_HEADS = 16
QK_NOPE = 128
QK_ROPE = 64
V_DIM = 128
ROPE_BASE = 10000.0
LN_EPS = 1e-5
RMS_EPS = 1e-6

VMEM_LIMIT_BYTES = 56 * 1024 * 1024
LANES = 128

ROW_TILE = 768
LATENT_TILE = 512
CONV_TILE = 256
CONV_CH_TILE = 2048
SCAN_TILE = 256
SCAN_GROUPS = 8
UP_HEADS_PER_STEP = 4
ATTN_Q_TILE = 1024
ATTN_K_TILE = 768
ATTN_SUB_ROWS = 256
ATTN_CHUNK_ROWS = 32


def _tile(limit, *sizes):
    return max(t for t in range(LANES, limit + 1, LANES) if all(sz % t == 0 for sz in sizes))


def _params(n_axes):
    return pltpu.CompilerParams(dimension_semantics=("arbitrary",) * n_axes,
                                vmem_limit_bytes=VMEM_LIMIT_BYTES)


def _silu(v):
    return v * (1.0 / (1.0 + jnp.exp(-v)))


def _softplus(v):
    return jnp.maximum(v, 0.0) + jnp.log(1.0 + jnp.exp(-jnp.abs(v)))


def _dot(a, b):
    return jnp.dot(a, b, preferred_element_type=F32)


def _dot_nt(a, b):
    return lax.dot_general(a, b, (((1,), (1,)), ((), ())), preferred_element_type=F32)


def _mod_kernel(c_ref, w_ref, b_ref, o_ref):
    a = _silu(c_ref[...])
    o_ref[...] = jnp.dot(a, w_ref[...], preferred_element_type=F32,
                         precision=lax.Precision.HIGHEST) + b_ref[...]


def _modulation(c_rows, ada_w, ada_b):
    depth, d, n = ada_w.shape
    tn = 768 if n % 768 == 0 else n
    return pl.pallas_call(
        _mod_kernel,
        grid=(depth, n // tn),
        in_specs=[pl.BlockSpec((8, d), lambda i, j: (0, 0)),
                  pl.BlockSpec((None, d, tn), lambda i, j: (i, 0, j)),
                  pl.BlockSpec((None, 1, tn), lambda i, j: (i, 0, j))],
        out_specs=pl.BlockSpec((None, 8, tn), lambda i, j: (i, 0, j)),
        out_shape=jax.ShapeDtypeStruct((depth, 8, n), F32),
        compiler_params=_params(2),
        name="adaln_modulation",
    )(c_rows, ada_w, ada_b.reshape(depth, 1, n))


def _modulate(x_ref, mod_ref, u_ref, row0, n_latent):
    tm = x_ref.shape[0]
    rows = row0 + lax.broadcasted_iota(jnp.int32, (tm, 1), 0)
    is_ctx = rows >= n_latent
    shift = jnp.where(is_ctx, mod_ref[0:1, :], mod_ref[3:4, :])
    scale = jnp.where(is_ctx, mod_ref[1:2, :], mod_ref[4:5, :])
    u_ref[...] = (x_ref[...] * (1.0 + scale) + shift).astype(BF16)


def _inproj_kernel(x_ref, mod_ref, w_ref, g_ref, cos_ref, sin_ref, o_ref, u_ref, *, n_latent, kinds):
    i, j = pl.program_id(1), pl.program_id(2)
    tm = x_ref.shape[0]

    @pl.when(j == 0)
    def _():
        _modulate(x_ref, mod_ref, u_ref, i * tm, n_latent)

    def tile(kind):
        acc = _dot(u_ref[...], w_ref[...])
        if kind == "silu":
            acc = _silu(acc)
        elif kind == "rmsnorm":
            ms = jnp.mean(acc * acc, axis=-1, keepdims=True)
            acc = acc * lax.rsqrt(ms + RMS_EPS) * g_ref[...]
        if kind == "rope":
            a = acc[:, :LANES]
            o_ref[:, :LANES] = (a * cos_ref[...] + pltpu.roll(a, 64, 1) * sin_ref[...]).astype(o_ref.dtype)
            if acc.shape[1] > LANES:
                o_ref[:, LANES:] = acc[:, LANES:].astype(o_ref.dtype)
        else:
            o_ref[...] = acc.astype(o_ref.dtype)

    for kind in dict.fromkeys(kinds):
        js = [t for t, kd in enumerate(kinds) if kd == kind]
        assert js == list(range(js[0], js[-1] + 1))
        pl.when(jnp.logical_and(j >= js[0], j <= js[-1]))(functools.partial(tile, kind))


def _inproj(x, mod, w, n_latent, *, tn, kinds, norm_g=None, tables=None, name):
    b, lt, d = x.shape
    n = w.shape[1]
    tm = ROW_TILE
    assert len(kinds) * tn == n
    if norm_g is None:
        norm_g = jnp.ones((1, n), F32)
    if tables is None:
        tables = (jnp.zeros((lt, LANES), F32),) * 2
    return pl.pallas_call(
        functools.partial(_inproj_kernel, n_latent=n_latent, kinds=tuple(kinds)),
        grid=(b, lt // tm, n // tn),
        in_specs=[pl.BlockSpec((None, tm, d), lambda bi, i, j: (bi, i, 0)),
                  pl.BlockSpec((None, 8, d), lambda bi, i, j: (bi, 0, 0)),
                  pl.BlockSpec((d, tn), lambda bi, i, j: (0, j)),
                  pl.BlockSpec((1, tn), lambda bi, i, j: (0, j)),
                  pl.BlockSpec((tm, LANES), lambda bi, i, j: (i, 0)),
                  pl.BlockSpec((tm, LANES), lambda bi, i, j: (i, 0))],
        out_specs=pl.BlockSpec((None, tm, tn), lambda bi, i, j: (bi, i, j)),
        out_shape=jax.ShapeDtypeStruct((b, lt, n), BF16),
        scratch_shapes=[pltpu.VMEM((tm, d), BF16)],
        compiler_params=_params(3),
        name=name,
    )(x, mod, w, norm_g, *tables)


def _dt_kernel(x_ref, mod_ref, w_ref, bias_ref, o_ref, u_ref, *, n_latent):
    i = pl.program_id(1)
    _modulate(x_ref, mod_ref, u_ref, i * x_ref.shape[0], n_latent)
    raw = _dot_nt(w_ref[...], u_ref[...])
    o_ref[...] = _softplus(raw + bias_ref[...])


def _dt_proj(x, mod, w_t, bias, n_latent):
    b, lt, d = x.shape
    nh = w_t.shape[0]
    tm = ROW_TILE
    return pl.pallas_call(
        functools.partial(_dt_kernel, n_latent=n_latent),
        grid=(b, lt // tm),
        in_specs=[pl.BlockSpec((None, tm, d), lambda bi, i: (bi, i, 0)),
                  pl.BlockSpec((None, 8, d), lambda bi, i: (bi, 0, 0)),
                  pl.BlockSpec((nh, d), lambda bi, i: (0, 0)),
                  pl.BlockSpec((nh, 1), lambda bi, i: (0, 0))],
        out_specs=pl.BlockSpec((None, nh, tm), lambda bi, i: (bi, 0, i)),
        out_shape=jax.ShapeDtypeStruct((b, nh, lt), F32),
        scratch_shapes=[pltpu.VMEM((tm, d), BF16)],
        compiler_params=_params(2),
        name="ssd_dt_proj",
    )(x, mod, w_t, bias)


def _conv_kernel(cur_ref, prev_ref, next_ref, w_ref, b_ref, o_ref, buf_ref, *, n_latent, n_rows):
    i = pl.program_id(1)
    tb = cur_ref.shape[0]
    taps = w_ref.shape[0]
    pad = taps // 2
    start = i * tb
    prev_ok = jnp.logical_and(start != 0, start != n_latent)
    next_ok = jnp.logical_and(start + tb != n_latent, start + tb != n_rows)
    buf_ref[0:8, :] = jnp.where(prev_ok, prev_ref[...].astype(F32), 0.0)
    buf_ref[8:8 + tb, :] = cur_ref[...].astype(F32)
    buf_ref[8 + tb:16 + tb, :] = jnp.where(next_ok, next_ref[...].astype(F32), 0.0)
    acc = b_ref[...] + w_ref[0:1, :] * buf_ref[8 - pad:8 - pad + tb, :]
    for k in range(1, taps):
        acc = acc + w_ref[k:k + 1, :] * buf_ref[8 - pad + k:8 - pad + k + tb, :]
    o_ref[...] = _silu(acc).astype(o_ref.dtype)


def _conv_silu(h, conv_w, conv_b, n_latent, col0):
    b, lt, _ = h.shape
    c = conv_w.shape[1]
    tb, cb = CONV_TILE, _tile(CONV_CH_TILE, c, col0)
    j0 = col0 // cb
    r8 = tb // 8
    last8 = lt // 8 - 1
    return pl.pallas_call(
        functools.partial(_conv_kernel, n_latent=n_latent, n_rows=lt),
        grid=(b, lt // tb, c // cb),
        in_specs=[pl.BlockSpec((None, tb, cb), lambda bi, i, j: (bi, i, j0 + j)),
                  pl.BlockSpec((None, 8, cb), lambda bi, i, j: (bi, jnp.maximum(i * r8 - 1, 0), j0 + j)),
                  pl.BlockSpec((None, 8, cb), lambda bi, i, j: (bi, jnp.minimum((i + 1) * r8, last8), j0 + j)),
                  pl.BlockSpec((conv_w.shape[0], cb), lambda bi, i, j: (0, j)),
                  pl.BlockSpec((1, cb), lambda bi, i, j: (0, j))],
        out_specs=pl.BlockSpec((None, tb, cb), lambda bi, i, j: (bi, i, j)),
        out_shape=jax.ShapeDtypeStruct((b, lt, c), BF16),
        scratch_shapes=[pltpu.VMEM((tb + 16, cb), F32)],
        compiler_params=_params(3),
        name="ssd_conv_silu",
    )(h, h, h, conv_w, conv_b.reshape(1, c))


def _scan_kernel(x_ref, b_ref, c_ref, dt_ref, a_ref, *rest, reverse, finish, n_groups):
    if finish:
        yf_ref, z_ref, dskip_ref, nw_ref, o_ref, h_ref = rest
    else:
        o_ref, h_ref = rest
    t = pl.program_id(2)
    q = SSD_CHUNK
    p = SSD_HEAD_DIM
    n = SSD_STATE
    gw = x_ref.shape[1] // n_groups
    n_heads = gw // p
    n_chunks = x_ref.shape[0] // q

    @pl.when(t == 0)
    def _():
        h_ref[...] = jnp.zeros_like(h_ref)

    li = lax.broadcasted_iota(jnp.int32, (q, q), 0)
    ri = lax.broadcasted_iota(jnp.int32, (q, q), 1)
    keep = (ri >= li) if reverse else (ri <= li)
    keep_t = (li >= ri) if reverse else (li <= ri)
    cum_mat = jnp.where(keep_t, 1.0, 0.0).astype(BF16)
    lane = lax.broadcasted_iota(jnp.int32, (q, 2 * p), 1)
    first_half = lane < p
    lo_mask = jnp.where(first_half, 1.0, 0.0).astype(BF16)
    hi_mask = jnp.where(first_half, 0.0, 1.0).astype(BF16)
    sel = (lax.broadcasted_iota(jnp.int32, (n_heads, gw), 1) // p
           == lax.broadcasted_iota(jnp.int32, (n_heads, gw), 0))

    order = range(n_chunks - 1, -1, -1) if reverse else range(n_chunks)
    for ck in order:
        rows = pl.ds(ck * q, q)
        for gi in range(n_groups):
            cols = slice(gi * gw, (gi + 1) * gw)
            hrows = slice(gi * n_heads, (gi + 1) * n_heads)
            xg = x_ref[rows, cols]
            bg = b_ref[rows, gi * n:(gi + 1) * n]
            cg = c_ref[rows, gi * n:(gi + 1) * n]
            dt_t = dt_ref[hrows, rows]
            a_t = dt_t * a_ref[hrows, :]
            a_hi = a_t.astype(BF16)
            r1 = a_t - a_hi.astype(F32)
            a_mid = r1.astype(BF16)
            a_lo = (r1 - a_mid.astype(F32)).astype(BF16)
            cs_t = _dot(a_hi, cum_mat) + _dot(a_mid, cum_mat) + _dot(a_lo, cum_mat)
            tot = jnp.sum(a_t, axis=1, keepdims=True)
            w_t = (dt_t * jnp.exp2(tot - cs_t)).astype(BF16)
            src_t = cs_t - jnp.log2(dt_t)
            decay_row = jnp.sum(jnp.where(sel, jnp.exp2(tot), 0.0), axis=0, keepdims=True)

            cb = _dot_nt(cg, bg).astype(BF16)
            bg_t = bg.astype(F32).T.astype(BF16)
            state = h_ref[gi]
            y_state = _dot(cg, state.astype(BF16))

            y_parts, inc_parts = [], []
            for pr in range(n_heads // 2):
                lhs_top, lhs_bot, cs_cols = [], [], []
                for e in (2 * pr, 2 * pr + 1):
                    cs_col = jnp.sum(jnp.where(keep, a_t[e:e + 1, :], 0.0), axis=1, keepdims=True)
                    seg = jnp.where(keep, cs_col - src_t[e:e + 1, :], -jnp.inf)
                    lhs_top.append(jnp.exp2(seg).astype(BF16) * cb)
                    lhs_bot.append(bg_t * w_t[e:e + 1, :])
                    cs_cols.append(cs_col)
                lhs = jnp.concatenate([jnp.concatenate(lhs_top, axis=1),
                                       jnp.concatenate(lhs_bot, axis=1)], axis=0)
                x_pair = xg[:, 2 * pr * p:(2 * pr + 2) * p]
                rhs = jnp.concatenate([x_pair * lo_mask, x_pair * hi_mask], axis=0)
                out = _dot(lhs, rhs)
                from_start = jnp.exp2(jnp.where(first_half, cs_cols[0], cs_cols[1]))
                y_parts.append(out[:q, :] + y_state[:, 2 * pr * p:(2 * pr + 2) * p] * from_start)
                inc_parts.append(out[q:, :])
            y = jnp.concatenate(y_parts, axis=1)
            h_ref[gi] = state * decay_row + jnp.concatenate(inc_parts, axis=1)

            if finish:
                y = y + yf_ref[rows, cols].astype(F32) + dskip_ref[:, cols] * xg.astype(F32)
                y = y * z_ref[rows, cols].astype(F32)
                ms = jnp.mean(y * y, axis=-1, keepdims=True)
                y = y * lax.rsqrt(ms + RMS_EPS) * nw_ref[:, cols]
            o_ref[rows, cols] = y.astype(o_ref.dtype)


def _ssd_scan(xbc, dt_t, a_cols, n_latent, *, reverse, finish_inputs=None):
    b, lt, _ = xbc.shape
    g, n, p = SSD_GROUPS, SSD_STATE, SSD_HEAD_DIM
    heads = dt_t.shape[1] // 2
    e = heads // g
    ng = SCAN_GROUPS
    gw = ng * e * p
    d_inner = heads * p
    tb = SCAN_TILE
    nblk = lt // tb
    n_lat_blk = n_latent // tb
    direction = 1 if reverse else 0
    gsteps = g // ng
    b_col0 = d_inner // (ng * n)

    def blk(t):
        if reverse:
            return nblk - 1 - t
        return jnp.where(t < nblk - n_lat_blk, n_lat_blk + t, t - (nblk - n_lat_blk))

    in_specs = [pl.BlockSpec((None, tb, gw), lambda bi, gi, t: (bi, blk(t), gi)),
                pl.BlockSpec((None, tb, ng * n), lambda bi, gi, t: (bi, blk(t), b_col0 + gi)),
                pl.BlockSpec((None, tb, ng * n), lambda bi, gi, t: (bi, blk(t), b_col0 + gsteps + gi)),
                pl.BlockSpec((None, ng * e, tb), lambda bi, gi, t: (bi, direction * gsteps + gi, blk(t))),
                pl.BlockSpec((ng * e, 1), lambda bi, gi, t: (direction * gsteps + gi, 0))]
    args = [xbc, xbc, xbc, dt_t, a_cols]
    if finish_inputs is not None:
        y_fwd, z_act, d_skip, norm_w = finish_inputs
        in_specs += [pl.BlockSpec((None, tb, gw), lambda bi, gi, t: (bi, blk(t), gi)),
                     pl.BlockSpec((None, tb, gw), lambda bi, gi, t: (bi, blk(t), gi)),
                     pl.BlockSpec((1, gw), lambda bi, gi, t: (0, gi)),
                     pl.BlockSpec((1, gw), lambda bi, gi, t: (0, gi))]
        args += [y_fwd, z_act, d_skip, norm_w]
    return pl.pallas_call(
        functools.partial(_scan_kernel, reverse=reverse, finish=finish_inputs is not None, n_groups=ng),
        grid=(b, gsteps, nblk),
        in_specs=in_specs,
        out_specs=pl.BlockSpec((None, tb, gw), lambda bi, gi, t: (bi, blk(t), gi)),
        out_shape=jax.ShapeDtypeStruct((b, lt, d_inner), BF16),
        scratch_shapes=[pltpu.VMEM((ng, n, e * p), F32)],
        compiler_params=_params(3),
        name="ssd_scan_bwd" if reverse else "ssd_scan_fwd",
    )(*args)


def _outproj_kernel(y_ref, w_ref, x_ref, mod_ref, g_ref, b_ref, o_ref, acc_ref, *, n_latent, alpha):
    i, k = pl.program_id(1), pl.program_id(2)
    tm = x_ref.shape[0]

    @pl.when(k == 0)
    def _():
        acc_ref[...] = jnp.zeros_like(acc_ref)

    acc_ref[...] += _dot(y_ref[...], w_ref[...])

    @pl.when(k == pl.num_programs(2) - 1)
    def _():
        rows = i * tm + lax.broadcasted_iota(jnp.int32, (tm, 1), 0)
        gate = jnp.where(rows >= n_latent, mod_ref[2:3, :], mod_ref[5:6, :])
        r = alpha * x_ref[...] + gate * acc_ref[...]
        mu = jnp.mean(r, axis=-1, keepdims=True)
        rc = r - mu
        var = jnp.mean(rc * rc, axis=-1, keepdims=True)
        o_ref[...] = rc * lax.rsqrt(var + LN_EPS) * g_ref[...] + b_ref[...]


def _outproj_norm(y, w, x, mod, ln_g, ln_b, n_latent, alpha, *, rows):
    b, _, d = x.shape
    kdim = y.shape[2]
    tm = ROW_TILE if rows % ROW_TILE == 0 else LATENT_TILE
    tk = min(512, kdim)
    return pl.pallas_call(
        functools.partial(_outproj_kernel, n_latent=n_latent, alpha=alpha),
        grid=(b, rows // tm, kdim // tk),
        in_specs=[pl.BlockSpec((None, tm, tk), lambda bi, i, k: (bi, i, k)),
                  pl.BlockSpec((tk, d), lambda bi, i, k: (k, 0)),
                  pl.BlockSpec((None, tm, d), lambda bi, i, k: (bi, i, 0)),
                  pl.BlockSpec((None, 8, d), lambda bi, i, k: (bi, 0, 0)),
                  pl.BlockSpec((1, d), lambda bi, i, k: (0, 0)),
                  pl.BlockSpec((1, d), lambda bi, i, k: (0, 0))],
        out_specs=pl.BlockSpec((None, tm, d), lambda bi, i, k: (bi, i, 0)),
        out_shape=jax.ShapeDtypeStruct((b, rows, d), F32),
        scratch_shapes=[pltpu.VMEM((tm, d), F32)],
        compiler_params=_params(3),
        name="outproj_deepnorm",
    )(y, w, x, mod, ln_g.reshape(1, d), ln_b.reshape(1, d))


def _qkvup_kernel(cq_ref, ckv_ref, wq_ref, wkv_ref, cos_ref, sin_ref, kr_ref, q_ref, k_ref, v_ref):
    cq = cq_ref[...]
    ckv = ckv_ref[...]
    for hh in range(wq_ref.shape[0]):
        acc = _dot(cq, wq_ref[hh])
        hi = acc[:, QK_NOPE:]
        q_ref[hh, :, :QK_NOPE] = acc[:, :QK_NOPE].astype(q_ref.dtype)
        q_ref[hh, :, QK_NOPE:] = (hi * cos_ref[...] + pltpu.roll(hi, 64, 1) * sin_ref[...]).astype(q_ref.dtype)
        acc = _dot(ckv, wkv_ref[hh])
        k_ref[hh, :, :QK_NOPE] = acc[:, :QK_NOPE].astype(k_ref.dtype)
        k_ref[hh, :, QK_NOPE:] = kr_ref[...]
        v_ref[hh, :, :V_DIM] = acc[:, QK_NOPE:].astype(v_ref.dtype)
        v_ref[hh, :, V_DIM:] = jnp.ones((acc.shape[0], V_DIM), v_ref.dtype)


def _qkv_up(cqkv, w_q, w_kv, cos_t, sin_t, kr_col):
    b, lt, _ = cqkv.shape
    h, r, wq = w_q.shape
    tm = ROW_TILE
    hb = min(UP_HEADS_PER_STEP, h)
    return pl.pallas_call(
        _qkvup_kernel,
        grid=(b, lt // tm, h // hb),
        in_specs=[pl.BlockSpec((None, tm, r), lambda bi, i, hi: (bi, i, 0)),
                  pl.BlockSpec((None, tm, r), lambda bi, i, hi: (bi, i, 1)),
                  pl.BlockSpec((hb, r, wq), lambda bi, i, hi: (hi, 0, 0)),
                  pl.BlockSpec((hb, r, wq), lambda bi, i, hi: (hi, 0, 0)),
                  pl.BlockSpec((tm, LANES), lambda bi, i, hi: (i, 0)),
                  pl.BlockSpec((tm, LANES), lambda bi, i, hi: (i, 0)),
                  pl.BlockSpec((None, tm, LANES), lambda bi, i, hi: (bi, i, kr_col))],
        out_specs=[pl.BlockSpec((None, hb, tm, wq), lambda bi, i, hi: (bi, hi, i, 0)),
                   pl.BlockSpec((None, hb, tm, 2 * LANES), lambda bi, i, hi: (bi, hi, i, 0)),
                   pl.BlockSpec((None, hb, tm, 2 * V_DIM), lambda bi, i, hi: (bi, hi, i, 0))],
        out_shape=[jax.ShapeDtypeStruct((b, h, lt, wq), BF16),
                   jax.ShapeDtypeStruct((b, h, lt, 2 * LANES), BF16),
                   jax.ShapeDtypeStruct((b, h, lt, 2 * V_DIM), BF16)],
        compiler_params=_params(3),
        name="mla_qkv_up",
    )(cqkv, cqkv, w_q, w_kv, cos_t, sin_t, cqkv)


def _attn_kernel(q_ref, k_ref, v_ref, g_ref, o_ref, s_ref, p_ref, m_ref, acc_ref, *, tk, sub):
    tq = q_ref.shape[0]
    n_k = k_ref.shape[0] // tk
    n_grp = tk // LANES
    n_sub = tq // sub
    m_ref[...] = jnp.full_like(m_ref, -jnp.inf)
    acc_ref[...] = jnp.zeros_like(acc_ref)

    def key_rows(t):
        return pl.ds(pl.multiple_of(t * tk, tk), tk)

    def scores(t, slot, i):
        r = slice(i * sub, (i + 1) * sub)
        s_ref[slot, r, :] = _dot_nt(q_ref[r, :], k_ref[key_rows(t), :])

    def update(t, slot, i):
        for c in range(sub // ATTN_CHUNK_ROWS):
            r = slice(i * sub + c * ATTN_CHUNK_ROWS, i * sub + (c + 1) * ATTN_CHUNK_ROWS)
            groups = [s_ref[slot, r, j * LANES:(j + 1) * LANES] for j in range(n_grp)]
            m_part = functools.reduce(jnp.maximum, groups)
            m_prev = m_ref[r, :]
            m_new = jnp.maximum(m_prev, jnp.max(m_part, axis=-1, keepdims=True))
            alpha = jnp.exp2(m_prev - m_new)
            for j, sg in enumerate(groups):
                p_ref[r, j * LANES:(j + 1) * LANES] = jnp.exp2(sg - m_new).astype(BF16)
            m_ref[r, :] = m_new
            acc_ref[r, :V_DIM] = alpha * acc_ref[r, :V_DIM]
            acc_ref[r, V_DIM:] = alpha * acc_ref[r, V_DIM:]
        r = slice(i * sub, (i + 1) * sub)
        acc_ref[r, :] += _dot(p_ref[r, :], v_ref[key_rows(t), :])

    def step(t, slot):
        for i in range(n_sub):
            scores(t + 1, 1 - slot, i)
            update(t, slot, i)

    for i in range(n_sub):
        scores(0, 0, i)

    for t in range(n_k - 1):
        step(t, t % 2)
    for i in range(n_sub):
        update(n_k - 1, (n_k - 1) % 2, i)
    o_ref[...] = (acc_ref[:, :V_DIM] / acc_ref[:, V_DIM:] * g_ref[...].astype(F32)).astype(o_ref.dtype)


def _attention_latent(qh, kh, vh, gate, gate_col, n_latent):
    b, h, lt, wq = qh.shape
    tq, tk = ATTN_Q_TILE, ATTN_K_TILE
    return pl.pallas_call(
        functools.partial(_attn_kernel, tk=tk, sub=min(ATTN_SUB_ROWS, tq)),
        grid=(b, h, n_latent // tq),
        in_specs=[pl.BlockSpec((None, None, tq, wq), lambda bi, hi, i: (bi, hi, i, 0)),
                  pl.BlockSpec((None, None, lt, wq), lambda bi, hi, i: (bi, hi, 0, 0)),
                  pl.BlockSpec((None, None, lt, 2 * V_DIM), lambda bi, hi, i: (bi, hi, 0, 0)),
                  pl.BlockSpec((None, tq, V_DIM), lambda bi, hi, i: (bi, i, gate_col + hi))],
        out_specs=pl.BlockSpec((None, tq, V_DIM), lambda bi, hi, i: (bi, i, hi)),
        out_shape=jax.ShapeDtypeStruct((b, n_latent, h * V_DIM), BF16),
        scratch_shapes=[pltpu.VMEM((2, tq, tk), F32), pltpu.VMEM((tq, tk), BF16),
                        pltpu.VMEM((tq, LANES), F32), pltpu.VMEM((tq, 2 * V_DIM), F32)],
        compiler_params=_params(3),
        name="mla_attention",
    )(qh, kh, vh, gate)


def _attn_ctx_kernel(q_ref, k_ref, v_ref, g_ref, o_ref):
    s = _dot_nt(q_ref[...], k_ref[...])
    m = jnp.max(s, axis=-1, keepdims=True)
    pexp = jnp.exp2(s - m)
    o = _dot(pexp.astype(BF16), v_ref[...])
    o_ref[...] = (o[:, :V_DIM] / o[:, V_DIM:] * g_ref[...].astype(F32)).astype(o_ref.dtype)


def _attention_ctx(qh, kh, vh, gate, gate_col, n_latent):
    b, h, lt, wq = qh.shape
    lc = lt - n_latent
    blk = n_latent // lc
    return pl.pallas_call(
        _attn_ctx_kernel,
        grid=(b, h),
        in_specs=[pl.BlockSpec((None, None, lc, wq), lambda bi, hi: (bi, hi, blk, 0)),
                  pl.BlockSpec((None, None, lc, wq), lambda bi, hi: (bi, hi, blk, 0)),
                  pl.BlockSpec((None, None, lc, 2 * V_DIM), lambda bi, hi: (bi, hi, blk, 0)),
                  pl.BlockSpec((None, lc, V_DIM), lambda bi, hi: (bi, blk, gate_col + hi))],
        out_specs=pl.BlockSpec((None, lc, V_DIM), lambda bi, hi: (bi, 0, hi)),
        out_shape=jax.ShapeDtypeStruct((b, lc, h * V_DIM), BF16),
        compiler_params=_params(2),
        name="mla_attention_ctx",
    )(qh, kh, vh, gate)


def _rope_tables(n_latent, n_ctx):
    t = jnp.arange(n_latent)
    rr = (t // GRID_W).astype(F32)
    cc = (t % GRID_W).astype(F32)
    n_freq = QK_ROPE // 4
    inv_freq = ROPE_BASE ** (-jnp.arange(n_freq, dtype=F32) / n_freq)
    ang = jnp.concatenate([rr[:, None] * inv_freq, cc[:, None] * inv_freq], -1)
    cos = jnp.repeat(jnp.cos(ang), 2, axis=-1)
    sin = jnp.repeat(jnp.sin(ang), 2, axis=-1)
    cos = jnp.concatenate([cos, jnp.ones((n_ctx, QK_ROPE), F32)], 0)
    sin = jnp.concatenate([sin, jnp.zeros((n_ctx, QK_ROPE), F32)], 0)
    pad = jnp.zeros((n_latent + n_ctx, LANES - QK_ROPE), F32)
    return jnp.concatenate([cos, pad], 1), jnp.concatenate([sin, pad], 1)


def _rotate_cols(w):
    wp = w.reshape(w.shape[:-1] + (w.shape[-1] // 2, 2))
    return jnp.stack([-wp[..., 1], wp[..., 0]], -1).reshape(w.shape)


def _ssd_layer(xs, mod, n_latent, w_in, conv_w, conv_b, dt_bias, a_log, d_skip, norm_w):
    heads = dt_bias.shape[-1]
    d_inner = heads * SSD_HEAD_DIM
    conv_ch = conv_w.shape[-1]
    w_zx = w_in[:, :d_inner + conv_ch].astype(BF16)
    w_dt_t = w_in[:, d_inner + conv_ch:].T.astype(BF16)
    tn = _tile(1024, d_inner, conv_ch)
    h_zx = _inproj(xs, mod, w_zx, n_latent, tn=tn, kinds=["silu"] * (d_inner // tn) + ["none"] * (conv_ch // tn),
                   name="ssd_in_zx")
    dt_t = _dt_proj(xs, mod, w_dt_t, dt_bias.reshape(2 * heads, 1), n_latent)
    xbc = _conv_silu(h_zx, conv_w, conv_b, n_latent, d_inner)
    a_cols = (-jnp.exp(a_log.astype(F32)) * math.log2(math.e)).reshape(2 * heads, 1)
    y_fwd = _ssd_scan(xbc, dt_t, a_cols, n_latent, reverse=False)
    d_row = jnp.repeat(d_skip, SSD_HEAD_DIM).reshape(1, d_inner)
    return _ssd_scan(xbc, dt_t, a_cols, n_latent, reverse=True,
                     finish_inputs=(y_fwd, h_zx, d_row, norm_w.reshape(1, d_inner)))


def _mla_layer(xs, mod, n_latent, w_in, q_norm, kv_norm, w_uq, w_ukv, tables, need_ctx):
    q_rank, kv_rank = q_norm.shape[0], kv_norm.shape[0]
    h = MLA_HEADS
    cos_t, sin_t = tables
    assert q_rank == kv_rank and q_rank % LANES == 0
    tn = q_rank
    w_kr = w_in[:, 2 * tn:2 * tn + QK_ROPE]
    w_gate = w_in[:, 2 * tn + QK_ROPE:]
    n_gate = w_gate.shape[1]
    w_all = jnp.concatenate([w_in[:, :2 * tn], w_kr, _rotate_cols(w_kr),
                             jnp.zeros((w_in.shape[0], tn - 2 * QK_ROPE), F32), w_gate], 1).astype(BF16)
    norm_g = jnp.concatenate([q_norm, kv_norm, jnp.ones((tn + n_gate,), F32)]).reshape(1, -1)
    h_all = _inproj(xs, mod, w_all, n_latent, tn=tn, kinds=["rmsnorm"] * 2 + ["rope"] + ["silu"] * (n_gate // tn),
                    norm_g=norm_g, tables=tables, name="mla_in_proj")
    kr_col, gate_col = 2 * tn // LANES, 3 * tn // V_DIM
    scale = (QK_NOPE + QK_ROPE) ** -0.5 * math.log2(math.e)
    wq = (w_uq * scale).reshape(q_rank, h, QK_NOPE + QK_ROPE).transpose(1, 0, 2)
    wq = jnp.concatenate([wq, _rotate_cols(wq[..., QK_NOPE:])], -1).astype(BF16)
    wkv = w_ukv.reshape(kv_rank, h, QK_NOPE + V_DIM).transpose(1, 0, 2).astype(BF16)
    qh, kh, vh = _qkv_up(h_all, wq, wkv, cos_t, sin_t, kr_col)
    o = _attention_latent(qh, kh, vh, h_all, gate_col, n_latent)
    if need_ctx:
        o = jnp.concatenate([o, _attention_ctx(qh, kh, vh, h_all, gate_col, n_latent)], axis=1)
    return o


def kernel(x, c, ctx, c_ctx, ada_w, ada_b, ln_g, ln_b, ssd_w_in, ssd_conv_w, ssd_conv_b, ssd_dt_bias, ssd_a_log, ssd_d, ssd_norm_w, ssd_w_out, mla_w_in, mla_q_norm, mla_kv_norm, mla_w_uq, mla_w_ukv, mla_w_o):
    depth = ada_w.shape[0]
    b, n_latent, d = x.shape
    n_ctx = ctx.shape[1]
    alpha = (2 * depth) ** 0.25
    assert b + 1 <= 8

    c_rows = jnp.concatenate([c, c_ctx[None, :], jnp.zeros((8 - b - 1, d), F32)], 0)
    mods = _modulation(c_rows, ada_w, ada_b)
    tables = _rope_tables(n_latent, n_ctx)
    xs = jnp.concatenate([x, ctx], axis=1)

    for i in range(depth):
        last = i == depth - 1
        m = mods[i].reshape(8, 3, d)
        ctx_mod = jnp.broadcast_to(m[b][None], (b, 3, d))
        mod = jnp.concatenate([ctx_mod, m[:b], jnp.zeros((b, 2, d), F32)], 1)
        j = i // 2
        if i % 2 == 0:
            y = _ssd_layer(xs, mod, n_latent, ssd_w_in[j], ssd_conv_w[j], ssd_conv_b[j], ssd_dt_bias[j],
                           ssd_a_log[j], ssd_d[j], ssd_norm_w[j])
            w_out = ssd_w_out[j].astype(BF16)
        else:
            y = _mla_layer(xs, mod, n_latent, mla_w_in[j], mla_q_norm[j], mla_kv_norm[j], mla_w_uq[j],
                           mla_w_ukv[j], tables, need_ctx=not last)
            w_out = mla_w_o[j].astype(BF16)
        rows = n_latent if last else n_latent + n_ctx
        xs = _outproj_norm(y, w_out, xs, mod, ln_g[i], ln_b[i], n_latent, alpha, rows=rows)
    return xs
```
